```python
import jax, jax.numpy as jnp
from jax import lax
import numpy as np

D_MODEL = 2048
BATCH = 2
SEQ = 4096
DEPTH = 4
DEC_BATCH = 8
DEC_SEQ = 32
PAST_LEN = 1024

CHUNK = 64
N_EVEN = (DEPTH + 1) // 2
N_ODD = DEPTH // 2
D_FF = 5632
EPS = 1e-6
D_A = D_MODEL // 2
CONV_A = 3
D_B = D_MODEL // 2
CONV_B = 31
EVEN_SPLITS = (D_A, D_A, D_A, D_B, D_B)
EVEN_IN = sum(EVEN_SPLITS)
DK_C = 128
DV_C = 128
H_C = D_MODEL // 256
D_C = H_C * DV_C
DK_D = 128
DV_D = 128
H_D = D_MODEL // 256
D_DK = H_D * DK_D
D_DV = H_D * DV_D
ODD_SPLITS = (H_C * DK_C, H_C * DK_C, D_C, D_C, D_DK, D_DK, D_DV, D_DV, H_C, H_C)
ODD_IN = sum(ODD_SPLITS)

kernel_name = "hybrid_streaming_conv_mlstm_hgrn2_step"

F32 = jnp.float32


def _split(z, sizes):
    idx = [int(v) for v in np.cumsum(sizes)[:-1]]
    return jnp.split(z, idx, axis=-1)


def rmsnorm(x, g):
    xf = x.astype(F32)
    y = xf * lax.rsqrt(jnp.mean(xf * xf, axis=-1, keepdims=True) + EPS)
    return (y * g.astype(F32)).astype(x.dtype)


def layernorm(x, g, b):
    xf = x.astype(F32)
    mu = jnp.mean(xf, axis=-1, keepdims=True)
    var = jnp.mean(jnp.square(xf - mu), axis=-1, keepdims=True)
    y = (xf - mu) * lax.rsqrt(var + EPS)
    return (y * g.astype(F32) + b.astype(F32)).astype(x.dtype)


def head_rmsnorm(h, g):
    y = h * lax.rsqrt(jnp.mean(h * h, axis=-1, keepdims=True) + EPS)
    return y.reshape(h.shape[0], h.shape[1], -1) * g.astype(F32)


def swiglu(h, w_gate, w_up, w_down):
    return (jax.nn.silu(h @ w_gate) * (h @ w_up)) @ w_down


def causal_dwconv(u, buf, w):
    full = jnp.concatenate([buf.astype(u.dtype), u], axis=1)
    out = lax.conv_general_dilated(full, w[:, None, :].astype(u.dtype), window_strides=(1,), padding='VALID',
                                   dimension_numbers=('NWC', 'WIO', 'NWC'), feature_group_count=u.shape[-1])
    return out, full[:, full.shape[1] - (w.shape[0] - 1):]


def _block_len(t):
    return CHUNK if t % CHUNK == 0 else t


def _to_blocks(a, nc, L):
    return a.reshape(a.shape[0], nc, L, *a.shape[2:]).swapaxes(0, 1)


def mlstm_chunkwise(q, k, v, log_i, log_f, c0, n0, m0):
    B, T, H, _ = q.shape
    L = _block_len(T)
    nc = T // L
    xs = tuple(_to_blocks(a, nc, L) for a in (q, k, v, log_i, log_f))
    causal = jnp.tril(jnp.ones((L, L), bool))

    def step(carry, xc):
        c, n, m = carry
        qc, kc, vc, ic, fc = xc
        bt = jnp.cumsum(fc, axis=1).transpose(0, 2, 1)
        it = ic.transpose(0, 2, 1)
        dmat = jnp.where(causal, bt[..., :, None] - bt[..., None, :] + it[..., None, :], -jnp.inf)
        inter = bt + m[..., None]
        m_row = jnp.maximum(inter, dmat.max(-1))
        s = jnp.einsum('blhd,bshd->bhls', qc, kc) * jnp.exp(dmat - m_row[..., None])
        w_inter = jnp.exp(inter - m_row).transpose(0, 2, 1)
        num = jnp.einsum('bhls,bshe->blhe', s, vc) + w_inter[..., None] * jnp.einsum('blhd,bhde->blhe', qc, c)
        den = s.sum(-1).transpose(0, 2, 1) + w_inter * jnp.einsum('blhd,bhd->blh', qc, n)
        h = num / jnp.maximum(jnp.abs(den), jnp.exp(-m_row).transpose(0, 2, 1))[..., None]
        b_last = bt[..., -1]
        g = b_last[..., None] - bt + it
        m_new = jnp.maximum(b_last + m, g.max(-1))
        w_s = jnp.exp(g - m_new[..., None])
        decay = jnp.exp(b_last + m - m_new)
        c_new = decay[..., None, None] * c + jnp.einsum('bhs,bshd,bshe->bhde', w_s, kc, vc)
        n_new = decay[..., None] * n + jnp.einsum('bhs,bshd->bhd', w_s, kc)
        return (c_new, n_new, m_new), h

    (c1, n1, m1), hs = lax.scan(step, (c0.astype(F32), n0.astype(F32), m0.astype(F32)), xs)
    return hs.swapaxes(0, 1).reshape(B, T, H, -1), c1, n1, m1


def hgrn2_chunkwise(q, k, v, log_f, s0):
    B, T, H, _ = q.shape
    L = _block_len(T)
    nc = T // L
    xs = tuple(_to_blocks(a, nc, L) for a in (q, k, v, log_f))
    causal = jnp.tril(jnp.ones((L, L), bool))[None, :, :, None, None]

    def step(s, xc):
        qc, kc, vc, gc = xc
        b = jnp.cumsum(gc, axis=1)
        diff = jnp.where(causal, b[:, :, None] - b[:, None, :], -jnp.inf)
        a = jnp.einsum('bthd,bshd,btshd->bhts', qc, kc, jnp.exp(diff))
        o = jnp.einsum('bhts,bshe->bthe', a, vc) + jnp.einsum('bthd,bhde->bthe', qc * jnp.exp(b), s)
        b_last = b[:, -1]
        s_new = jnp.exp(b_last)[..., None] * s + jnp.einsum('bshd,bshe->bhde', kc * jnp.exp(b_last[:, None] - b), vc)
        return s_new, o

    s1, os_ = lax.scan(step, s0.astype(F32), xs)
    return os_.swapaxes(0, 1).reshape(B, T, H, -1), s1


def even_mixer(h, buf_a, buf_b, w_in, conv_a, conv_b, conv_b_bias, ln_g, ln_b, w_out):
    xa, gate_b, gate_c, glu_v, glu_g = _split(h @ w_in, EVEN_SPLITS)
    ca, new_a = causal_dwconv(gate_c * xa, buf_a, conv_a)
    y_a = gate_b * ca
    cb, new_b = causal_dwconv(glu_v * jax.nn.sigmoid(glu_g), buf_b, conv_b)
    y_b = jax.nn.silu(layernorm(cb + conv_b_bias.astype(cb.dtype), ln_g, ln_b))
    return jnp.concatenate([y_a, y_b], axis=-1) @ w_out, new_a, new_b


def odd_mixer(h, c0, n0, m0, s0, w_in, bias_i, bias_f, norm_c, lb, norm_d, w_out):
    B, T, _ = h.shape
    q_c, k_c, v_c, o_c, q_d, f_d, i_d, g_d, i_c, f_c = _split(h @ w_in, ODD_SPLITS)
    heads = lambda a, n: a.astype(F32).reshape(B, T, n, -1)
    log_i = i_c.astype(F32) + bias_i.astype(F32)
    log_f = jax.nn.log_sigmoid(f_c.astype(F32) + bias_f.astype(F32))
    hc, c1, n1, m1 = mlstm_chunkwise(heads(q_c, H_C), heads(k_c, H_C) * (DK_C ** -0.5), heads(v_c, H_C),
                                     log_i, log_f, c0, n0, m0)
    y_c = jax.nn.sigmoid(o_c) * head_rmsnorm(hc, norm_c).astype(h.dtype)
    lbh = lb.astype(F32).reshape(H_D, DK_D)
    forget = lbh + (1.0 - lbh) * jax.nn.sigmoid(heads(f_d, H_D))
    hd, s1 = hgrn2_chunkwise(jax.nn.silu(heads(q_d, H_D)), 1.0 - forget, heads(i_d, H_D), jnp.log(forget), s0)
    y_d = jax.nn.sigmoid(g_d) * head_rmsnorm(hd, norm_d).astype(h.dtype)
    return jnp.concatenate([y_c, y_d], axis=-1) @ w_out, c1, n1, m1, s1


def _trunk(x, st_a, st_b, st_c, st_n, st_m, st_s, norm_g, norm_f, ffn_w_gate, ffn_w_up, ffn_w_down,
           even_w_in, even_conv_a, even_conv_b, even_conv_b_bias, even_ln_g, even_ln_b, even_w_out,
           odd_w_in, odd_bias_i, odd_bias_f, odd_norm_c, lb_all, odd_norm_d, odd_w_out):
    na, nb, ncs, nns, nms, nss = [], [], [], [], [], []
    for l in range(DEPTH):
        j = l // 2
        x = x + 0.5 * swiglu(rmsnorm(x, norm_g[l, 0]), ffn_w_gate[l, 0], ffn_w_up[l, 0], ffn_w_down[l, 0])
        h = rmsnorm(x, norm_g[l, 1])
        if l % 2 == 0:
            y, a1, b1 = even_mixer(h, st_a[j], st_b[j], even_w_in[j], even_conv_a[j], even_conv_b[j],
                                   even_conv_b_bias[j], even_ln_g[j], even_ln_b[j], even_w_out[j])
            na.append(a1)
            nb.append(b1)
        else:
            y, c1, n1, m1, s1 = odd_mixer(h, st_c[j], st_n[j], st_m[j], st_s[j], odd_w_in[j], odd_bias_i[j],
                                          odd_bias_f[j], odd_norm_c[j], lb_all[j], odd_norm_d[j], odd_w_out[j])
            ncs.append(c1)
            nns.append(n1)
            nms.append(m1)
            nss.append(s1)
        x = x + y.astype(x.dtype)
        x = x + 0.5 * swiglu(rmsnorm(x, norm_g[l, 2]), ffn_w_gate[l, 1], ffn_w_up[l, 1], ffn_w_down[l, 1])
    return (rmsnorm(x, norm_f), jnp.stack(na), jnp.stack(nb), jnp.stack(ncs), jnp.stack(nns),
            jnp.stack(nms), jnp.stack(nss))


def setup_inputs(seed: int = 0) -> dict:
    key = jax.random.key(seed)
    ks = jax.random.split(key, 32)
    nrm = lambda k, shape, s: jax.random.normal(k, shape, F32) * s
    f_bias = jnp.linspace(3.0, 6.0, H_C, dtype=F32)[None, :] + nrm(ks[20], (N_ODD, H_C), 0.1)
    return {
        "x_prompt": nrm(ks[0], (BATCH, SEQ, D_MODEL), 1.0),
        "x_sample": nrm(ks[1], (DEC_BATCH, DEC_SEQ, D_MODEL), 1.0),
        "state_conv_a": nrm(ks[2], (N_EVEN, DEC_BATCH, CONV_A - 1, D_A), 1.0),
        "state_conv_b": nrm(ks[3], (N_EVEN, DEC_BATCH, CONV_B - 1, D_B), 1.0),
        "state_mlstm_c": nrm(ks[4], (N_ODD, DEC_BATCH, H_C, DK_C, DV_C), 0.1),
        "state_mlstm_n": nrm(ks[5], (N_ODD, DEC_BATCH, H_C, DK_C), 0.1),
        "state_mlstm_m": nrm(ks[6], (N_ODD, DEC_BATCH, H_C), 0.5),
        "state_hgrn_s": nrm(ks[7], (N_ODD, DEC_BATCH, H_D, DK_D, DV_D), 0.5),
        "norm_g": 1.0 + nrm(ks[8], (DEPTH, 3, D_MODEL), 0.02),
        "norm_f": 1.0 + nrm(ks[9], (D_MODEL,), 0.02),
        "ffn_w_gate": nrm(ks[10], (DEPTH, 2, D_MODEL, D_FF), D_MODEL ** -0.5),
        "ffn_w_up": nrm(ks[11], (DEPTH, 2, D_MODEL, D_FF), D_MODEL ** -0.5),
        "ffn_w_down": nrm(ks[12], (DEPTH, 2, D_FF, D_MODEL), D_FF ** -0.5),
        "even_w_in": nrm(ks[13], (N_EVEN, D_MODEL, EVEN_IN), D_MODEL ** -0.5),
        "even_conv_a": nrm(ks[14], (N_EVEN, CONV_A, D_A), CONV_A ** -0.5),
        "even_conv_b": nrm(ks[15], (N_EVEN, CONV_B, D_B), CONV_B ** -0.5),
        "even_conv_b_bias": nrm(ks[16], (N_EVEN, D_B), 0.02),
        "even_ln_g": 1.0 + nrm(ks[17], (N_EVEN, D_B), 0.02),
        "even_ln_b": nrm(ks[18], (N_EVEN, D_B), 0.02),
        "even_w_out": nrm(ks[19], (N_EVEN, D_A + D_B, D_MODEL), (D_A + D_B) ** -0.5),
        "odd_w_in": nrm(ks[21], (N_ODD, D_MODEL, ODD_IN), D_MODEL ** -0.5),
        "odd_bias_i": nrm(ks[22], (N_ODD, H_C), 0.1),
        "odd_bias_f": f_bias,
        "odd_norm_c": 1.0 + nrm(ks[23], (N_ODD, D_C), 0.02),
        "odd_lb_logits": nrm(ks[24], (N_ODD, D_DK), 0.1),
        "odd_norm_d": 1.0 + nrm(ks[25], (N_ODD, D_DV), 0.02),
        "odd_w_out": nrm(ks[26], (N_ODD, D_C + D_DV, D_MODEL), (D_C + D_DV) ** -0.5),
    }


def reference(x_prompt, x_sample, state_conv_a, state_conv_b, state_mlstm_c, state_mlstm_n, state_mlstm_m,
              state_hgrn_s, norm_g, norm_f, ffn_w_gate, ffn_w_up, ffn_w_down, even_w_in, even_conv_a,
              even_conv_b, even_conv_b_bias, even_ln_g, even_ln_b, even_w_out, odd_w_in, odd_bias_i,
              odd_bias_f, odd_norm_c, odd_lb_logits, odd_norm_d, odd_w_out):
    lb_sm = jax.nn.softmax(odd_lb_logits.astype(F32), axis=0)
    lb_all = jnp.cumsum(lb_sm, axis=0) - lb_sm[0]
    weights = (norm_g, norm_f, ffn_w_gate, ffn_w_up, ffn_w_down, even_w_in, even_conv_a, even_conv_b,
               even_conv_b_bias, even_ln_g, even_ln_b, even_w_out, odd_w_in, odd_bias_i, odd_bias_f,
               odd_norm_c, lb_all, odd_norm_d, odd_w_out)
    z_a = jnp.zeros((N_EVEN, BATCH, CONV_A - 1, D_A), x_prompt.dtype)
    z_b = jnp.zeros((N_EVEN, BATCH, CONV_B - 1, D_B), x_prompt.dtype)
    z_c = jnp.zeros((N_ODD, BATCH, H_C, DK_C, DV_C), F32)
    z_n = jnp.zeros((N_ODD, BATCH, H_C, DK_C), F32)
    z_m = jnp.zeros((N_ODD, BATCH, H_C), F32)
    z_s = jnp.zeros((N_ODD, BATCH, H_D, DK_D, DV_D), F32)
    y_prompt, p_conv_a, p_conv_b, p_mlstm_c, p_mlstm_n, p_mlstm_m, p_hgrn_s = _trunk(
        x_prompt, z_a, z_b, z_c, z_n, z_m, z_s, *weights)
    y_sample, s_conv_a, s_conv_b, s_mlstm_c, s_mlstm_n, s_mlstm_m, s_hgrn_s = _trunk(
        x_sample, state_conv_a, state_conv_b, state_mlstm_c, state_mlstm_n, state_mlstm_m, state_hgrn_s,
        *weights)
    return (y_prompt, y_sample, p_conv_a, p_conv_b, p_mlstm_c, p_mlstm_n, p_mlstm_m, p_hgrn_s,
            s_conv_a, s_conv_b, s_mlstm_c, s_mlstm_n, s_mlstm_m, s_hgrn_s)
```

```python
import functools

import jax
import jax.numpy as jnp
from jax import lax
from jax.experimental import pallas as pl
from jax.experimental.pallas import tpu as pltpu

F32 = jnp.float32
BF16 = jnp.bfloat16
EPS = 1e-6

V7X_LANES = 128
V7X_SUBLANES = 8
V7X_VMEM_LIMIT = 56 * 1024 * 1024
HEAD_DIM = 128
ROW_TILE_CAP = 768
FF_TILE_CAP = 512
COL_TILE_CAP = 1024
CONV_TILE_CAP = 256
CHUNK_CAP = 128
SUB = 16


def _tile(n, cap, align):
    if n <= cap:
        return n
    for d in range(cap - cap % align, 0, -align):
        if n % d == 0:
            return d
    raise ValueError(f"no tile for {n} (cap {cap}, align {align})")


def _params(*sem):
    return pltpu.CompilerParams(dimension_semantics=sem, vmem_limit_bytes=V7X_VMEM_LIMIT)


def _rms(x, g):
    return x * lax.rsqrt(jnp.mean(x * x, axis=-1, keepdims=True) + EPS) * g


def _dot(a, b):
    return jnp.dot(a, b, preferred_element_type=F32)


def _dot_nt(a, b):
    return lax.dot_general(a, b, (((1,), (1,)), ((), ())), preferred_element_type=F32)


def _dot_tn(a, b):
    return lax.dot_general(a, b, (((0,), (0,)), ((), ())), preferred_element_type=F32)


def _split3(x):
    hi = x.astype(BF16)
    r1 = x - hi.astype(F32)
    mid = r1.astype(BF16)
    lo = (r1 - mid.astype(F32)).astype(BF16)
    return hi, mid, lo


def _log_sigmoid(x):
    return jnp.minimum(x, 0.0) - jnp.log1p(jnp.exp(-jnp.abs(x)))


def _ffn_body(x_ref, g_ref, wg_ref, wu_ref, wd_ref, *rest, final_norm):
    if final_norm:
        gf_ref, o_ref, xn_ref, acc_ref = rest
    else:
        o_ref, xn_ref, acc_ref = rest
    j = pl.program_id(1)

    @pl.when(j == 0)
    def _():
        xn_ref[...] = _rms(x_ref[...], g_ref[...]).astype(BF16)
        acc_ref[...] = jnp.zeros_like(acc_ref)

    xn = xn_ref[...]
    hg = _dot(xn, wg_ref[...])
    hu = _dot(xn, wu_ref[...])
    a = (hg * jax.nn.sigmoid(hg) * hu).astype(BF16)
    acc_ref[...] += _dot(a, wd_ref[...])

    @pl.when(j == pl.num_programs(1) - 1)
    def _():
        y = x_ref[...] + 0.5 * acc_ref[...]
        if final_norm:
            y = _rms(y, gf_ref[...])
        o_ref[...] = y


def _ffn(x, g, wg, wu, wd, l, k, gf=None):
    m, d = x.shape
    f = wg.shape[-1]
    tm = _tile(m, ROW_TILE_CAP, V7X_LANES)
    tf = _tile(f, FF_TILE_CAP, V7X_LANES)
    in_specs = [
        pl.BlockSpec((tm, d), lambda i, j: (i, 0)),
        pl.BlockSpec((1, d), lambda i, j: (0, 0)),
        pl.BlockSpec((None, None, d, tf), lambda i, j: (l, k, 0, j)),
        pl.BlockSpec((None, None, d, tf), lambda i, j: (l, k, 0, j)),
        pl.BlockSpec((None, None, tf, d), lambda i, j: (l, k, j, 0)),
    ]
    args = [x, g, wg, wu, wd]
    if gf is not None:
        in_specs.append(pl.BlockSpec((1, d), lambda i, j: (0, 0)))
        args.append(gf)
    return pl.pallas_call(
        functools.partial(_ffn_body, final_norm=gf is not None),
        grid=(m // tm, f // tf),
        in_specs=in_specs,
        out_specs=pl.BlockSpec((tm, d), lambda i, j: (i, 0)),
        out_shape=jax.ShapeDtypeStruct((m, d), F32),
        scratch_shapes=[pltpu.VMEM((tm, d), BF16), pltpu.VMEM((tm, d), F32)],
        compiler_params=_params("parallel", "arbitrary"),
        name="ffn_final" if gf is not None else "ffn",
    )(*args)


def _proj_body(x_ref, g_ref, w_ref, *rest, gates):
    if gates:
        wgc_ref, wgr_ref, z_ref, gc_ref, gr_ref, xn_ref = rest
    else:
        z_ref, xn_ref = rest

    @pl.when(pl.program_id(1) == 0)
    def _():
        xn = _rms(x_ref[...], g_ref[...]).astype(BF16)
        xn_ref[...] = xn
        if gates:
            gc_ref[...] = _dot(xn, wgc_ref[...])
            gr_ref[...] = _dot_nt(wgr_ref[...], xn)

    z_ref[...] = _dot(xn_ref[...], w_ref[...])


def _proj(x, g, w, jl, wgc=None, wgr=None):
    m, d = x.shape
    n = w.shape[-1]
    tm = _tile(m, ROW_TILE_CAP, V7X_LANES)
    tn = _tile(n, COL_TILE_CAP, V7X_LANES)
    gates = wgc is not None
    in_specs = [
        pl.BlockSpec((tm, d), lambda i, j: (i, 0)),
        pl.BlockSpec((1, d), lambda i, j: (0, 0)),
        pl.BlockSpec((None, d, tn), lambda i, j: (jl, 0, j)),
    ]
    out_specs = [pl.BlockSpec((tm, tn), lambda i, j: (i, j))]
    out_shape = [jax.ShapeDtypeStruct((m, n), F32)]
    args = [x, g, w]
    if gates:
        gr = wgr.shape[1]
        in_specs += [
            pl.BlockSpec((None, d, V7X_LANES), lambda i, j: (jl, 0, 0)),
            pl.BlockSpec((None, gr, d), lambda i, j: (jl, 0, 0)),
        ]
        out_specs += [
            pl.BlockSpec((tm, V7X_LANES), lambda i, j: (i, 0)),
            pl.BlockSpec((gr, tm), lambda i, j: (0, i)),
        ]
        out_shape += [jax.ShapeDtypeStruct((m, V7X_LANES), F32), jax.ShapeDtypeStruct((gr, m), F32)]
        args += [wgc, wgr]
    outs = pl.pallas_call(
        functools.partial(_proj_body, gates=gates),
        grid=(m // tm, n // tn),
        in_specs=in_specs,
        out_specs=out_specs,
        out_shape=out_shape,
        scratch_shapes=[pltpu.VMEM((tm, d), BF16)],
        compiler_params=_params("parallel", "arbitrary"),
        name="proj_gates" if gates else "proj",
    )(*args)
    return outs if gates else outs[0]


def _outproj_body(x_ref, y_ref, w_ref, o_ref):
    o_ref[...] = x_ref[...] + _dot(y_ref[...], w_ref[...])


def _outproj(x, y, w, jl):
    m, d = x.shape
    kdim = y.shape[1]
    tm = _tile(m, ROW_TILE_CAP, V7X_LANES)
    tn = _tile(d, COL_TILE_CAP, V7X_LANES)
    return pl.pallas_call(
        _outproj_body,
        grid=(m // tm, d // tn),
        in_specs=[
            pl.BlockSpec((tm, tn), lambda i, j: (i, j)),
            pl.BlockSpec((tm, kdim), lambda i, j: (i, 0)),
            pl.BlockSpec((None, kdim, tn), lambda i, j: (jl, 0, j)),
        ],
        out_specs=pl.BlockSpec((tm, tn), lambda i, j: (i, j)),
        out_shape=jax.ShapeDtypeStruct((m, d), F32),
        compiler_params=_params("parallel", "parallel"),
        name="outproj",
    )(x, y, w)


def _even_body(xa_ref, gb_ref, gc_ref, gv_ref, gg_ref, ha_ref, hb_ref, wa_ref, wb_ref, bias_ref, lng_ref,
               lnb_ref, y_ref, na_ref, nb_ref, ua_scr, ub_scr, *, tc, wa_len, wb_len, ha, hb):
    t = pl.program_id(1)
    c = xa_ref.shape[1]

    @pl.when(t == 0)
    def _():
        ua_scr[0:ha, :] = ha_ref[...]
        ub_scr[0:hb, :] = hb_ref[...]

    @pl.when(t > 0)
    def _():
        ua_scr[0:ha, :] = ua_scr[tc:tc + ha, :]
        ub_scr[0:hb, :] = ub_scr[tc:tc + hb, :]

    ua_scr[ha:ha + tc, :] = gc_ref[...] * xa_ref[...]
    ub_scr[hb:hb + tc, :] = gv_ref[...] * jax.nn.sigmoid(gg_ref[...])

    ca = jnp.zeros((tc, c), F32)
    for j in range(wa_len):
        ca = ca + wa_ref[j:j + 1, :] * ua_scr[ha - (wa_len - 1) + j: ha - (wa_len - 1) + j + tc, :]
    y_ref[:, 0:c] = (gb_ref[...] * ca).astype(y_ref.dtype)

    cb = jnp.zeros((tc, c), F32)
    for j in range(wb_len):
        cb = cb + wb_ref[j:j + 1, :] * ub_scr[hb - (wb_len - 1) + j: hb - (wb_len - 1) + j + tc, :]
    cb = cb + bias_ref[...]
    mu = jnp.mean(cb, axis=-1, keepdims=True)
    dc = cb - mu
    var = jnp.mean(dc * dc, axis=-1, keepdims=True)
    ln = dc * lax.rsqrt(var + EPS) * lng_ref[...] + lnb_ref[...]
    y_ref[:, c:2 * c] = (ln * jax.nn.sigmoid(ln)).astype(y_ref.dtype)

    @pl.when(t == pl.num_programs(1) - 1)
    def _():
        na_ref[...] = ua_scr[ha + tc - (wa_len - 1): ha + tc, :]
        nb_ref[...] = ub_scr[hb + tc - (wb_len - 1): hb + tc, :]


def _even_mixer(z, row_off, n_seq, t_len, hist_a, hist_b, conv_a, conv_b, bias, ln_g, ln_b):
    c = conv_a.shape[-1]
    wa_len, wb_len = conv_a.shape[0], conv_b.shape[0]
    ha, hb = hist_a.shape[1], hist_b.shape[1]
    tc = _tile(t_len, CONV_TILE_CAP, V7X_SUBLANES)
    nt = t_len // tc
    assert row_off % tc == 0 and tc >= hb and tc >= ha
    blk0 = row_off // tc

    def zspec(col):
        return pl.BlockSpec((tc, c), lambda s, t: (blk0 + s * nt + t, col))

    def whole(a):
        return pl.BlockSpec(a.shape, lambda s, t: (0,) * a.ndim)

    return pl.pallas_call(
        functools.partial(_even_body, tc=tc, wa_len=wa_len, wb_len=wb_len, ha=ha, hb=hb),
        grid=(n_seq, nt),
        in_specs=[zspec(0), zspec(1), zspec(2), zspec(3), zspec(4),
                  pl.BlockSpec((None, ha, c), lambda s, t: (s, 0, 0)),
                  pl.BlockSpec((None, hb, c), lambda s, t: (s, 0, 0)),
                  whole(conv_a), whole(conv_b), whole(bias), whole(ln_g), whole(ln_b)],
        out_specs=[pl.BlockSpec((tc, 2 * c), lambda s, t: (s * nt + t, 0)),
                   pl.BlockSpec((None, wa_len - 1, c), lambda s, t: (s, 0, 0)),
                   pl.BlockSpec((None, wb_len - 1, c), lambda s, t: (s, 0, 0))],
        out_shape=[jax.ShapeDtypeStruct((n_seq * t_len, 2 * c), BF16),
                   jax.ShapeDtypeStruct((n_seq, wa_len - 1, c), F32),
                   jax.ShapeDtypeStruct((n_seq, wb_len - 1, c), F32)],
        scratch_shapes=[pltpu.VMEM((ha + tc, c), F32), pltpu.VMEM((hb + tc, c), F32)],
        compiler_params=_params("arbitrary", "arbitrary"),
        name="even_mixer",
    )(z, z, z, z, z, hist_a, hist_b, conv_a, conv_b, bias, ln_g, ln_b)


def _lb_body(logit_ref, lb_ref):
    n = logit_ref.shape[0]
    rows = [logit_ref[j:j + 1, :] for j in range(n)]
    mx = functools.reduce(jnp.maximum, rows)
    es = [jnp.exp(r - mx) for r in rows]
    tot = functools.reduce(lambda a, b: a + b, es)
    first = es[0] / tot
    run = jnp.zeros_like(first)
    for j in range(n):
        run = run + es[j] / tot
        lb_ref[j:j + 1, :] = run - first


def _lower_bounds(logits):
    return pl.pallas_call(
        _lb_body,
        out_shape=jax.ShapeDtypeStruct(logits.shape, F32),
        name="hgrn_lower_bounds",
    )(logits)


def _odd_body(qc_ref, kc_ref, vc_ref, oc_ref, qd_ref, fd_ref, id_ref, gd_ref, gcol_ref, grow_ref,
              bcol_ref, brow_ref, nc_ref, nd_ref, lb_ref, c0_ref, n0_ref, m0_ref, s0_ref,
              y_ref, c1_ref, n1_ref, m1_ref, s1_ref, c_scr, n_scr, m_scr, s_scr, *, heads, blk):
    L = blk
    hd_all = heads * HEAD_DIM
    step = pl.program_id(1)

    @pl.when(step == 0)
    def _():
        c_scr[...] = c0_ref[...]
        n_scr[...] = n0_ref[...]
        m_scr[...] = m0_ref[...]
        s_scr[...] = s0_ref[...]

    row = lax.broadcasted_iota(jnp.int32, (L, L), 0)
    col = lax.broadcasted_iota(jnp.int32, (L, L), 1)
    causal = row >= col
    tri_l = causal.astype(BF16)
    tri_u = (row <= col).astype(BF16)

    def cumsum_rows(x):
        hi, mid, lo = _split3(x)
        return _dot(tri_l, hi) + _dot(tri_l, mid) + _dot(tri_l, lo)

    def cumsum_lanes(x):
        hi, mid, lo = _split3(x)
        return _dot(hi, tri_u) + _dot(mid, tri_u) + _dot(lo, tri_u)

    gcol = gcol_ref[...] + bcol_ref[...]
    grow = grow_ref[...] + brow_ref[...]
    bt_col = cumsum_rows(_log_sigmoid(gcol))
    bt_row = cumsum_lanes(_log_sigmoid(grow))
    scale = HEAD_DIM ** -0.5

    for h in range(heads):
        hs = slice(h * HEAD_DIM, (h + 1) * HEAD_DIM)
        q = qc_ref[:, hs]
        k = kc_ref[:, hs] * scale
        vb = vc_ref[:, hs].astype(BF16)
        qb = q.astype(BF16)
        kb = k.astype(BF16)
        b_c = bt_col[:, heads + h:heads + h + 1]
        i_c = gcol[:, h:h + 1]
        b_r = bt_row[heads + h:heads + h + 1, :]
        i_r = grow[h:h + 1, :]
        m_prev = m_scr[h:h + 1, 0:1]
        dmat = jnp.where(causal, b_c - b_r + i_r, -jnp.inf)
        inter = b_c + m_prev
        m_row = jnp.maximum(inter, jnp.max(dmat, axis=-1, keepdims=True))
        s_mat = _dot_nt(qb, kb) * jnp.exp(dmat - m_row)
        w_inter = jnp.exp(inter - m_row)
        c_old = c_scr[h]
        n_old = n_scr[h:h + 1, :]
        num = _dot(s_mat.astype(BF16), vb) + w_inter * _dot(qb, c_old.astype(BF16))
        den = jnp.sum(s_mat, axis=-1, keepdims=True) + w_inter * jnp.sum(q * n_old, axis=-1, keepdims=True)
        hh = num / jnp.maximum(jnp.abs(den), jnp.exp(-m_row))
        hn = hh * lax.rsqrt(jnp.mean(hh * hh, axis=-1, keepdims=True) + EPS) * nc_ref[:, hs]
        y_ref[:, hs] = (jax.nn.sigmoid(oc_ref[:, hs]) * hn).astype(y_ref.dtype)
        b_last = b_r[:, L - 1:L]
        g_r = b_last - b_r + i_r
        g_c = b_last - b_c + i_c
        m_new = jnp.maximum(b_last + m_prev, jnp.max(g_r, axis=-1, keepdims=True))
        kw = k * jnp.exp(g_c - m_new)
        decay = jnp.exp(b_last + m_prev - m_new)
        c_scr[h] = decay * c_old + _dot_tn(kw.astype(BF16), vb)
        n_scr[h:h + 1, :] = decay * n_old + jnp.sum(kw, axis=0, keepdims=True)
        m_scr[h:h + 1, :] = jnp.broadcast_to(m_new, (1, m_scr.shape[1]))

    lb = lb_ref[...]
    f = lb + (1.0 - lb) * jax.nn.sigmoid(fd_ref[...])
    b_all = cumsum_rows(jnp.log(f))
    kk_all = 1.0 - f
    qd = qd_ref[...]
    qq_all = qd * jax.nn.sigmoid(qd)
    b_end = b_all[L - 1:L, :]
    q_in = (qq_all * jnp.exp(b_all)).astype(BF16)
    k_out = (kk_all * jnp.exp(b_end - b_all)).astype(BF16)
    s_decay = jnp.exp(b_end)
    row_s = lax.broadcasted_iota(jnp.int32, (SUB, HEAD_DIM), 0)
    row_l = lax.broadcasted_iota(jnp.int32, (L, HEAD_DIM), 0)
    col_l = lax.broadcasted_iota(jnp.int32, (SUB, L), 1)

    for h in range(heads):
        hs = slice(h * HEAD_DIM, (h + 1) * HEAD_DIM)
        b = b_all[:, hs]
        kk = kk_all[:, hs]
        qq = qq_all[:, hs]
        vb = id_ref[:, hs].astype(BF16)
        st = s_scr[h]
        o_inter = _dot_nt(q_in[:, hs], st.astype(BF16))
        s_scr[h] = st * s_decay[:, hs] + _dot_tn(vb, k_out[:, hs])
        blocks = []
        for bi in range(L // SUB):
            lo = bi * SUB
            b_i = b[lo:lo + SUB, :]
            q_i = qq[lo:lo + SUB, :]
            k_i = kk[lo:lo + SUB, :]
            a_blk = jnp.zeros((SUB, L), F32)
            for s in range(SUB):
                dec = jnp.exp(jnp.where(row_s >= s, b_i - b_i[s:s + 1, :], -jnp.inf))
                a_s = jnp.sum(q_i * k_i[s:s + 1, :] * dec, axis=-1, keepdims=True)
                a_blk = jnp.where(col_l == lo + s, a_s, a_blk)
            if bi > 0:
                ref_b = b[lo - 1:lo, :]
                q_t = (q_i * jnp.exp(b_i - ref_b)).astype(BF16)
                k_t = (kk * jnp.exp(jnp.where(row_l < lo, ref_b - b, -jnp.inf))).astype(BF16)
                a_blk = a_blk + _dot_nt(q_t, k_t)
            blocks.append(a_blk)
        a_mat = jnp.concatenate(blocks, axis=0) if len(blocks) > 1 else blocks[0]
        o = _dot(a_mat.astype(BF16), vb) + o_inter
        on = o * lax.rsqrt(jnp.mean(o * o, axis=-1, keepdims=True) + EPS) * nd_ref[:, hs]
        y_ref[:, hd_all + h * HEAD_DIM: hd_all + (h + 1) * HEAD_DIM] = (
            jax.nn.sigmoid(gd_ref[:, hs]) * on).astype(y_ref.dtype)

    @pl.when(step == pl.num_programs(1) - 1)
    def _():
        c1_ref[...] = c_scr[...]
        n1_ref[...] = n_scr[...]
        m1_ref[...] = m_scr[...]
        s1_ref[...] = s_scr[...]


def _odd_mixer(z, gcol, grow3, row_off, n_seq, t_len, blk, bcol, brow, norm_c, norm_d, lb, c0, n0, m0, s0t):
    heads = c0.shape[1]
    hd_all = heads * HEAD_DIM
    nblk = t_len // blk
    assert row_off % blk == 0 and blk % SUB == 0
    blk0 = row_off // blk
    gr = grow3.shape[1]

    def zspec(colblk):
        return pl.BlockSpec((blk, hd_all), lambda s, c: (blk0 + s * nblk + c, colblk))

    def whole(a):
        return pl.BlockSpec(a.shape, lambda s, c: (0,) * a.ndim)

    def per_seq(a):
        return pl.BlockSpec((None,) + a.shape[1:], lambda s, c: (s,) + (0,) * (a.ndim - 1))

    state_shapes = [jax.ShapeDtypeStruct(a.shape, F32) for a in (c0, n0, m0, s0t)]
    return pl.pallas_call(
        functools.partial(_odd_body, heads=heads, blk=blk),
        grid=(n_seq, nblk),
        in_specs=[zspec(i) for i in range(8)] + [
            pl.BlockSpec((blk, V7X_LANES), lambda s, c: (blk0 + s * nblk + c, 0)),
            pl.BlockSpec((None, gr, blk), lambda s, c: (s * nblk + c, 0, 0)),
            whole(bcol), whole(brow), whole(norm_c), whole(norm_d), whole(lb),
            per_seq(c0), per_seq(n0), per_seq(m0), per_seq(s0t)],
        out_specs=[pl.BlockSpec((blk, 2 * hd_all), lambda s, c: (s * nblk + c, 0)),
                   per_seq(c0), per_seq(n0), per_seq(m0), per_seq(s0t)],
        out_shape=[jax.ShapeDtypeStruct((n_seq * t_len, 2 * hd_all), BF16)] + state_shapes,
        scratch_shapes=[pltpu.VMEM(c0.shape[1:], F32), pltpu.VMEM(n0.shape[1:], F32),
                        pltpu.VMEM(m0.shape[1:], F32), pltpu.VMEM(s0t.shape[1:], F32)],
        compiler_params=_params("arbitrary", "arbitrary"),
        name="odd_mixer",
    )(z, z, z, z, z, z, z, z, gcol, grow3, bcol, brow, norm_c, norm_d, lb, c0, n0, m0, s0t)


def kernel(x_prompt, x_sample, state_conv_a, state_conv_b, state_mlstm_c, state_mlstm_n, state_mlstm_m,
           state_hgrn_s, norm_g, norm_f, ffn_w_gate, ffn_w_up, ffn_w_down, even_w_in, even_conv_a,
           even_conv_b, even_conv_b_bias, even_ln_g, even_ln_b, even_w_out, odd_w_in, odd_bias_i,
           odd_bias_f, odd_norm_c, odd_lb_logits, odd_norm_d, odd_w_out):
    bp, tp, d = x_prompt.shape
    bs, ts, _ = x_sample.shape
    mp, ms = bp * tp, bs * ts
    depth = norm_g.shape[0]
    c_a, c_b = even_conv_a.shape[-1], even_conv_b.shape[-1]
    wa_len, wb_len = even_conv_a.shape[1], even_conv_b.shape[1]
    heads = state_mlstm_c.shape[2]
    hd_all = heads * HEAD_DIM
    assert c_a == c_b and even_w_in.shape[-1] == 5 * c_a
    assert state_mlstm_c.shape[3:] == (HEAD_DIM, HEAD_DIM) and state_hgrn_s.shape[2:] == (heads, HEAD_DIM, HEAD_DIM)
    assert odd_w_in.shape[-1] == 8 * hd_all + 2 * heads and 2 * heads <= V7X_LANES

    x = jnp.concatenate([x_prompt.reshape(mp, d), x_sample.reshape(ms, d)], axis=0)

    wg, wu, wd = ffn_w_gate.astype(BF16), ffn_w_up.astype(BF16), ffn_w_down.astype(BF16)
    ew_in, ew_out = even_w_in.astype(BF16), even_w_out.astype(BF16)
    ow_in, ow_out = odd_w_in[:, :, :8 * hd_all].astype(BF16), odd_w_out.astype(BF16)
    w_gate_cols = odd_w_in[:, :, 8 * hd_all:]
    gr = -(-2 * heads // V7X_SUBLANES) * V7X_SUBLANES
    wgc = jnp.pad(w_gate_cols, ((0, 0), (0, 0), (0, V7X_LANES - 2 * heads))).astype(BF16)
    wgr = jnp.pad(jnp.swapaxes(w_gate_cols, 1, 2), ((0, 0), (0, gr - 2 * heads), (0, 0))).astype(BF16)
    gate_bias = jnp.concatenate([odd_bias_i, odd_bias_f], axis=-1).astype(F32)
    bcol = jnp.pad(gate_bias, ((0, 0), (0, V7X_LANES - 2 * heads)))[:, None, :]
    brow = jnp.pad(gate_bias, ((0, 0), (0, gr - 2 * heads)))[:, :, None]
    lb_all = _lower_bounds(odd_lb_logits.astype(F32))

    ha = -(-(wa_len - 1) // V7X_SUBLANES) * V7X_SUBLANES
    hb = -(-(wb_len - 1) // V7X_SUBLANES) * V7X_SUBLANES

    def hist(state, n, h_rows, w_len):
        if state is None:
            return jnp.zeros((n, h_rows, c_a), F32)
        return jnp.pad(state.astype(F32), ((0, 0), (h_rows - (w_len - 1), 0), (0, 0)))

    blk_p = _tile(tp, CHUNK_CAP, SUB)
    blk_s = _tile(ts, CHUNK_CAP, SUB)
    groups = ((0, bp, tp, blk_p), (mp, bs, ts, blk_s))

    new_a, new_b, new_c, new_n, new_m, new_s = ([[], []] for _ in range(6))
    for l in range(depth):
        j = l // 2
        x = _ffn(x, norm_g[l, 0][None, :], wg, wu, wd, l, 0)
        if l % 2 == 0:
            z = _proj(x, norm_g[l, 1][None, :], ew_in, j)
            ys = []
            for gi, (off, n, t_len, _) in enumerate(groups):
                sa = None if gi == 0 else state_conv_a[j]
                sb = None if gi == 0 else state_conv_b[j]
                y, a1, b1 = _even_mixer(
                    z, off, n, t_len, hist(sa, n, ha, wa_len), hist(sb, n, hb, wb_len),
                    even_conv_a[j].astype(F32), even_conv_b[j].astype(F32),
                    even_conv_b_bias[j][None, :], even_ln_g[j][None, :], even_ln_b[j][None, :])
                ys.append(y)
                new_a[gi].append(a1)
                new_b[gi].append(b1)
            x = _outproj(x, jnp.concatenate(ys, axis=0), ew_out, j)
        else:
            z, gcol, grow = _proj(x, norm_g[l, 1][None, :], ow_in, j, wgc, wgr)
            ys = []
            for gi, (off, n, t_len, blk) in enumerate(groups):
                grow3 = grow[:, off:off + n * t_len].reshape(gr, n * t_len // blk, blk).transpose(1, 0, 2)
                if gi == 0:
                    c0 = jnp.zeros((n, heads, HEAD_DIM, HEAD_DIM), F32)
                    n0 = jnp.zeros((n, heads, HEAD_DIM), F32)
                    m0 = jnp.zeros((n, heads, V7X_LANES), F32)
                    s0t = jnp.zeros((n, heads, HEAD_DIM, HEAD_DIM), F32)
                else:
                    c0 = state_mlstm_c[j].astype(F32)
                    n0 = state_mlstm_n[j].astype(F32)
                    m0 = jnp.broadcast_to(state_mlstm_m[j].astype(F32)[:, :, None], (n, heads, V7X_LANES))
                    s0t = jnp.swapaxes(state_hgrn_s[j].astype(F32), -1, -2)
                y, c1, n1, m1, s1t = _odd_mixer(
                    z, gcol, grow3, off, n, t_len, blk, bcol[j], brow[j], odd_norm_c[j][None, :],
                    odd_norm_d[j][None, :], lb_all[j][None, :], c0, n0, m0, s0t)
                ys.append(y)
                new_c[gi].append(c1)
                new_n[gi].append(n1)
                new_m[gi].append(m1[:, :, 0])
                new_s[gi].append(jnp.swapaxes(s1t, -1, -2))
            x = _outproj(x, jnp.concatenate(ys, axis=0), ow_out, j)
        x = _ffn(x, norm_g[l, 2][None, :], wg, wu, wd, l, 1, gf=norm_f[None, :] if l == depth - 1 else None)

    y_prompt = x[:mp].reshape(bp, tp, d)
    y_sample = x[mp:].reshape(bs, ts, d)
    st = lambda lst, gi: jnp.stack(lst[gi])
    return (y_prompt, y_sample,
            st(new_a, 0), st(new_b, 0), st(new_c, 0), st(new_n, 0), st(new_m, 0), st(new_s, 0),
            st(new_a, 1), st(new_b, 1), st(new_c, 1), st(new_n, 1), st(new_m, 1), st(new_s, 1))
```

```python
import functools

import jax
import jax.numpy as jnp
from jax import lax
from jax.experimental import pallas as pl
from jax.experimental.pallas import tpu as pltpu

F32 = jnp.float32
BF16 = jnp.bfloat16
EPS = 1e-6

V7X_LANES = 128
V7X_SUBLANES = 8
V7X_VMEM_LIMIT = 56 * 1024 * 1024
HEAD_DIM = 128
ROW_TILE_CAP = 768
FF_TILE_CAP = 512
COL_TILE_CAP = 1024
CONV_TILE_CAP = 256
CHUNK_CAP = 128


def _tile(n, cap, align):
    if n <= cap:
        return n
    for d in range(cap - cap % align, 0, -align):
        if n % d == 0:
            return d
    raise ValueError(f"no tile for {n} (cap {cap}, align {align})")


def _pow2_block(n, cap):
    b = V7X_SUBLANES
    assert n % b == 0
    while 2 * b <= cap and n % (2 * b) == 0:
        b *= 2
    return b


def _params(*sem):
    return pltpu.CompilerParams(dimension_semantics=sem, vmem_limit_bytes=V7X_VMEM_LIMIT)


def _rms(x, g):
    return x * lax.rsqrt(jnp.mean(x * x, axis=-1, keepdims=True) + EPS) * g


def _dot(a, b):
    return jnp.dot(a, b, preferred_element_type=F32)


def _dot_nt(a, b):
    return lax.dot_general(a, b, (((1,), (1,)), ((), ())), preferred_element_type=F32)


def _dot_tn(a, b):
    return lax.dot_general(a, b, (((0,), (0,)), ((), ())), preferred_element_type=F32)


def _split3(x):
    hi = x.astype(BF16)
    r1 = x - hi.astype(F32)
    mid = r1.astype(BF16)
    lo = (r1 - mid.astype(F32)).astype(BF16)
    return hi, mid, lo


def _log_sigmoid(x):
    return jnp.minimum(x, 0.0) - jnp.log1p(jnp.exp(-jnp.abs(x)))


def _ffn_body(x_ref, g_ref, wg_ref, wu_ref, wd_ref, *rest, final_norm, cast_next):
    rest = list(rest)
    gf_ref = rest.pop(0) if final_norm else None
    nxt_in = [rest.pop(0) for _ in range(3)] if cast_next else []
    o_ref = rest.pop(0)
    nxt_out = [rest.pop(0) for _ in range(3)] if cast_next else []
    (xn_ref,) = rest
    j = pl.program_id(1)

    @pl.when(j == 0)
    def _():
        xn_ref[...] = _rms(x_ref[...], g_ref[...]).astype(BF16)
        o_ref[...] = jnp.zeros_like(o_ref)

    xn = xn_ref[...]
    hg = _dot(xn, wg_ref[...])
    hu = _dot(xn, wu_ref[...])
    a = (hg * jax.nn.sigmoid(hg) * hu).astype(BF16)
    o_ref[...] += _dot(a, wd_ref[...])

    @pl.when(j == pl.num_programs(1) - 1)
    def _():
        y = x_ref[...] + 0.5 * o_ref[...]
        if final_norm:
            y = _rms(y, gf_ref[...])
        o_ref[...] = y

    for src, dst in zip(nxt_in, nxt_out):
        dst[...] = src[...].astype(BF16)


def _ffn(x, g, wg, wu, wd, gf=None, nxt=None):
    m, d = x.shape
    f = wg.shape[-1]
    tm = _tile(m, ROW_TILE_CAP, V7X_LANES)
    tf = _tile(f, FF_TILE_CAP, V7X_LANES)
    n_i = m // tm
    in_specs = [
        pl.BlockSpec((tm, d), lambda i, j: (i, 0)),
        pl.BlockSpec((1, d), lambda i, j: (0, 0)),
        pl.BlockSpec((d, tf), lambda i, j: (0, j)),
        pl.BlockSpec((d, tf), lambda i, j: (0, j)),
        pl.BlockSpec((tf, d), lambda i, j: (j, 0)),
    ]
    out_specs = [pl.BlockSpec((tm, d), lambda i, j: (i, 0))]
    out_shape = [jax.ShapeDtypeStruct((m, d), F32)]
    args = [x, g, wg, wu, wd]
    if gf is not None:
        in_specs.append(pl.BlockSpec((1, d), lambda i, j: (0, 0)))
        args.append(gf)
    if nxt is not None:
        ng, nu, nd, ln, kn = nxt
        rb = next(r for r in range(V7X_LANES, d + 1, V7X_LANES) if d % r == 0 and d // r <= n_i)
        last_r, last_c = d // rb - 1, f // tf - 1

        def blk_r(i, j):
            return jnp.minimum(i, last_r)

        def blk_c(i, j):
            return jnp.where(i > last_r, last_c, j)

        in_specs += [
            pl.BlockSpec((None, None, rb, tf), lambda i, j: (ln, kn, blk_r(i, j), blk_c(i, j))),
            pl.BlockSpec((None, None, rb, tf), lambda i, j: (ln, kn, blk_r(i, j), blk_c(i, j))),
            pl.BlockSpec((None, None, tf, rb), lambda i, j: (ln, kn, blk_c(i, j), blk_r(i, j))),
        ]
        out_specs += [
            pl.BlockSpec((rb, tf), lambda i, j: (blk_r(i, j), blk_c(i, j))),
            pl.BlockSpec((rb, tf), lambda i, j: (blk_r(i, j), blk_c(i, j))),
            pl.BlockSpec((tf, rb), lambda i, j: (blk_c(i, j), blk_r(i, j))),
        ]
        out_shape += [jax.ShapeDtypeStruct((d, f), BF16), jax.ShapeDtypeStruct((d, f), BF16),
                      jax.ShapeDtypeStruct((f, d), BF16)]
        args += [ng, nu, nd]
    outs = pl.pallas_call(
        functools.partial(_ffn_body, final_norm=gf is not None, cast_next=nxt is not None),
        grid=(n_i, f // tf),
        in_specs=in_specs,
        out_specs=out_specs,
        out_shape=out_shape,
        scratch_shapes=[pltpu.VMEM((tm, d), BF16)],
        compiler_params=_params("arbitrary", "arbitrary"),
        name="ffn_final" if gf is not None else "ffn",
    )(*args)
    return outs[0], tuple(outs[1:])


def _proj_body(x_ref, g_ref, w_ref, *rest, gates):
    if gates:
        wgc_ref, wgr_ref, z_ref, gc_ref, gr_ref, xn_ref = rest
    else:
        z_ref, xn_ref = rest

    @pl.when(pl.program_id(1) == 0)
    def _():
        xn = _rms(x_ref[...], g_ref[...]).astype(BF16)
        xn_ref[...] = xn
        if gates:
            gc_ref[...] = _dot(xn, wgc_ref[...])
            gr_ref[...] = _dot_nt(wgr_ref[...], xn)

    z_ref[...] = _dot(xn_ref[...], w_ref[...].astype(BF16))


def _proj(x, g, w, jl, n, wgc=None, wgr=None):
    m, d = x.shape
    tm = _tile(m, ROW_TILE_CAP, V7X_LANES)
    tn = _tile(n, COL_TILE_CAP, V7X_LANES)
    gates = wgc is not None
    in_specs = [
        pl.BlockSpec((tm, d), lambda i, j: (i, 0)),
        pl.BlockSpec((1, d), lambda i, j: (0, 0)),
        pl.BlockSpec((None, d, tn), lambda i, j: (jl, 0, j)),
    ]
    out_specs = [pl.BlockSpec((tm, tn), lambda i, j: (i, j))]
    out_shape = [jax.ShapeDtypeStruct((m, n), F32)]
    args = [x, g, w]
    if gates:
        gr = wgr.shape[1]
        in_specs += [
            pl.BlockSpec((None, d, V7X_LANES), lambda i, j: (jl, 0, 0)),
            pl.BlockSpec((None, gr, d), lambda i, j: (jl, 0, 0)),
        ]
        out_specs += [
            pl.BlockSpec((tm, V7X_LANES), lambda i, j: (i, 0)),
            pl.BlockSpec((gr, tm), lambda i, j: (0, i)),
        ]
        out_shape += [jax.ShapeDtypeStruct((m, V7X_LANES), F32), jax.ShapeDtypeStruct((gr, m), F32)]
        args += [wgc, wgr]
    outs = pl.pallas_call(
        functools.partial(_proj_body, gates=gates),
        grid=(m // tm, n // tn),
        in_specs=in_specs,
        out_specs=out_specs,
        out_shape=out_shape,
        scratch_shapes=[pltpu.VMEM((tm, d), BF16)],
        compiler_params=_params("parallel", "arbitrary"),
        name="proj_gates" if gates else "proj",
    )(*args)
    return outs if gates else outs[0]


def _outproj_body(x_ref, y_ref, w_ref, o_ref):
    o_ref[...] = x_ref[...] + _dot(y_ref[...], w_ref[...].astype(BF16))


def _outproj(x, y, w, jl):
    m, d = x.shape
    kdim = y.shape[1]
    tm = _tile(m, ROW_TILE_CAP, V7X_LANES)
    tn = _tile(d, COL_TILE_CAP, V7X_LANES)
    return pl.pallas_call(
        _outproj_body,
        grid=(m // tm, d // tn),
        in_specs=[
            pl.BlockSpec((tm, tn), lambda i, j: (i, j)),
            pl.BlockSpec((tm, kdim), lambda i, j: (i, 0)),
            pl.BlockSpec((None, kdim, tn), lambda i, j: (jl, 0, j)),
        ],
        out_specs=pl.BlockSpec((tm, tn), lambda i, j: (i, j)),
        out_shape=jax.ShapeDtypeStruct((m, d), F32),
        compiler_params=_params("parallel", "parallel"),
        name="outproj",
    )(x, y, w)


def _even_body(xa_ref, gb_ref, gc_ref, gv_ref, gg_ref, ha_ref, hb_ref, wa_ref, wb_ref, bias_ref, lng_ref,
               lnb_ref, y_ref, na_ref, nb_ref, ua_scr, ub_scr, sh_scr, *, tc, wa_len, wb_len, ha, hb):
    t = pl.program_id(1)
    c = xa_ref.shape[1]

    @pl.when(t == 0)
    def _():
        ua_scr[0:ha, :] = ha_ref[...]
        ub_scr[0:hb, :] = hb_ref[...]

    @pl.when(t > 0)
    def _():
        ua_scr[0:ha, :] = ua_scr[tc:tc + ha, :]
        ub_scr[0:hb, :] = ub_scr[tc:tc + hb, :]

    ua_scr[ha:ha + tc, :] = gc_ref[...] * xa_ref[...]
    ub_scr[hb:hb + tc, :] = gv_ref[...] * jax.nn.sigmoid(gg_ref[...])

    def conv(w_ref, u_scr, first):
        n_taps = w_ref.shape[0]
        acc = jnp.zeros((tc, c), F32)
        for r in range(V7X_SUBLANES):
            offs = [o for o in range(first, first + n_taps) if o % V7X_SUBLANES == r]
            if not offs:
                continue
            span = max(offs) - r
            if r == 0:
                src = u_scr
            else:
                sh_scr[0:span + tc, :] = u_scr[r:r + span + tc, :]
                src = sh_scr
            for o in offs:
                acc = acc + w_ref[o - first:o - first + 1, :] * src[o - r:o - r + tc, :]
        return acc

    ca = conv(wa_ref, ua_scr, ha - (wa_len - 1))
    y_ref[:, 0:c] = (gb_ref[...] * ca).astype(y_ref.dtype)

    cb = conv(wb_ref, ub_scr, hb - (wb_len - 1))
    cb = cb + bias_ref[...]
    mu = jnp.mean(cb, axis=-1, keepdims=True)
    dc = cb - mu
    var = jnp.mean(dc * dc, axis=-1, keepdims=True)
    ln = dc * lax.rsqrt(var + EPS) * lng_ref[...] + lnb_ref[...]
    y_ref[:, c:2 * c] = (ln * jax.nn.sigmoid(ln)).astype(y_ref.dtype)

    @pl.when(t == pl.num_programs(1) - 1)
    def _():
        na_ref[...] = ua_scr[ha + tc - (wa_len - 1): ha + tc, :]
        nb_ref[...] = ub_scr[hb + tc - (wb_len - 1): hb + tc, :]


def _even_mixer(z, row_off, n_seq, t_len, hist_a, hist_b, conv_a, conv_b, bias, ln_g, ln_b):
    c = conv_a.shape[-1]
    wa_len, wb_len = conv_a.shape[0], conv_b.shape[0]
    ha, hb = hist_a.shape[1], hist_b.shape[1]
    tc = _tile(t_len, CONV_TILE_CAP, V7X_SUBLANES)
    nt = t_len // tc
    assert row_off % tc == 0 and tc >= hb and tc >= ha
    blk0 = row_off // tc

    def zspec(col):
        return pl.BlockSpec((tc, c), lambda s, t: (blk0 + s * nt + t, col))

    def whole(a):
        return pl.BlockSpec(a.shape, lambda s, t: (0,) * a.ndim)

    return pl.pallas_call(
        functools.partial(_even_body, tc=tc, wa_len=wa_len, wb_len=wb_len, ha=ha, hb=hb),
        grid=(n_seq, nt),
        in_specs=[zspec(0), zspec(1), zspec(2), zspec(3), zspec(4),
                  pl.BlockSpec((None, ha, c), lambda s, t: (s, 0, 0)),
                  pl.BlockSpec((None, hb, c), lambda s, t: (s, 0, 0)),
                  whole(conv_a), whole(conv_b), whole(bias), whole(ln_g), whole(ln_b)],
        out_specs=[pl.BlockSpec((tc, 2 * c), lambda s, t: (s * nt + t, 0)),
                   pl.BlockSpec((None, wa_len - 1, c), lambda s, t: (s, 0, 0)),
                   pl.BlockSpec((None, wb_len - 1, c), lambda s, t: (s, 0, 0))],
        out_shape=[jax.ShapeDtypeStruct((n_seq * t_len, 2 * c), BF16),
                   jax.ShapeDtypeStruct((n_seq, wa_len - 1, c), F32),
                   jax.ShapeDtypeStruct((n_seq, wb_len - 1, c), F32)],
        scratch_shapes=[pltpu.VMEM((ha + tc, c), F32), pltpu.VMEM((hb + tc, c), F32),
                        pltpu.VMEM((max(ha, hb) + tc, c), F32)],
        compiler_params=_params("arbitrary", "arbitrary"),
        name="even_mixer",
    )(z, z, z, z, z, hist_a, hist_b, conv_a, conv_b, bias, ln_g, ln_b)


def _lb_body(logit_ref, lb_ref):
    n = logit_ref.shape[0]
    rows = [logit_ref[j:j + 1, :] for j in range(n)]
    mx = functools.reduce(jnp.maximum, rows)
    es = [jnp.exp(r - mx) for r in rows]
    tot = functools.reduce(lambda a, b: a + b, es)
    first = es[0] / tot
    run = jnp.zeros_like(first)
    for j in range(n):
        run = run + es[j] / tot
        lb_ref[j:j + 1, :] = run - first


def _lower_bounds(logits):
    return pl.pallas_call(
        _lb_body,
        out_shape=jax.ShapeDtypeStruct(logits.shape, F32),
        name="hgrn_lower_bounds",
    )(logits)


def _odd_body(qc_ref, kc_ref, vc_ref, oc_ref, qd_ref, fd_ref, id_ref, gd_ref, gcol_ref, grow_ref,
              bcol_ref, brow_ref, nc_ref, nd_ref, lb_ref, c0_ref, n0_ref, m0_ref, s0_ref,
              y_ref, c1_ref, n1_ref, m1_ref, s1_ref, c_scr, n_scr, m_scr, s_scr, *, heads, blk):
    L = blk
    hd_all = heads * HEAD_DIM
    step = pl.program_id(1)

    @pl.when(step == 0)
    def _():
        c_scr[...] = c0_ref[...]
        n_scr[...] = n0_ref[...]
        m_scr[...] = m0_ref[...]
        s_scr[...] = s0_ref[...]

    row = lax.broadcasted_iota(jnp.int32, (L, L), 0)
    col = lax.broadcasted_iota(jnp.int32, (L, L), 1)
    causal = row >= col
    tri_l = causal.astype(BF16)
    tri_u = (row <= col).astype(BF16)

    def cumsum_rows(x):
        hi, mid, lo = _split3(x)
        return _dot(tri_l, hi) + _dot(tri_l, mid) + _dot(tri_l, lo)

    def cumsum_lanes(x):
        hi, mid, lo = _split3(x)
        return _dot(hi, tri_u) + _dot(mid, tri_u) + _dot(lo, tri_u)

    gcol = gcol_ref[...] + bcol_ref[...]
    grow = grow_ref[...] + brow_ref[...]
    bt_col = cumsum_rows(_log_sigmoid(gcol))
    bt_row = cumsum_lanes(_log_sigmoid(grow))
    scale = HEAD_DIM ** -0.5
    m_all = m_scr[...]
    n_all = n_scr[...]
    hr = range(heads)
    hsl = [slice(h * HEAD_DIM, (h + 1) * HEAD_DIM) for h in hr]
    b_c = [bt_col[:, heads + h:heads + h + 1] for h in hr]
    i_c = [gcol[:, h:h + 1] for h in hr]
    b_r = [bt_row[heads + h:heads + h + 1, :] for h in hr]
    i_r = [grow[h:h + 1, :] for h in hr]
    m_prev = [m_all[h:h + 1, 0:1] for h in hr]
    dmat = [jnp.where(causal, b_c[h] - b_r[h] + i_r[h], -jnp.inf) for h in hr]
    inter = [b_c[h] + m_prev[h] for h in hr]
    m_row = [jnp.maximum(inter[h], jnp.max(dmat[h], axis=-1, keepdims=True)) for h in hr]
    q = [qc_ref[:, hsl[h]] for h in hr]
    k = [kc_ref[:, hsl[h]] * scale for h in hr]
    qb = [q[h].astype(BF16) for h in hr]
    vb = [vc_ref[:, hsl[h]].astype(BF16) for h in hr]
    s_mat = [_dot_nt(qb[h], k[h].astype(BF16)) * jnp.exp(dmat[h] - m_row[h]) for h in hr]
    w_inter = [jnp.exp(inter[h] - m_row[h]) for h in hr]
    n_old = [n_all[h:h + 1, :] for h in hr]
    den = [jnp.sum(s_mat[h], axis=-1, keepdims=True)
           + w_inter[h] * jnp.sum(q[h] * n_old[h], axis=-1, keepdims=True) for h in hr]
    c_old = [c_scr[h] for h in hr]
    num = [_dot(s_mat[h].astype(BF16), vb[h]) + w_inter[h] * _dot(qb[h], c_old[h].astype(BF16)) for h in hr]
    hh = [num[h] / jnp.maximum(jnp.abs(den[h]), jnp.exp(-m_row[h])) for h in hr]
    ms = [jnp.mean(hh[h] * hh[h], axis=-1, keepdims=True) for h in hr]
    for h in hr:
        hn = hh[h] * lax.rsqrt(ms[h] + EPS) * nc_ref[:, hsl[h]]
        y_ref[:, hsl[h]] = (jax.nn.sigmoid(oc_ref[:, hsl[h]]) * hn).astype(y_ref.dtype)
    b_last = [b_r[h][:, L - 1:L] for h in hr]
    m_new = [jnp.maximum(b_last[h] + m_prev[h], jnp.max(b_last[h] - b_r[h] + i_r[h], axis=-1, keepdims=True))
             for h in hr]
    kw = [k[h] * jnp.exp(b_last[h] - b_c[h] + i_c[h] - m_new[h]) for h in hr]
    decay = [jnp.exp(b_last[h] + m_prev[h] - m_new[h]) for h in hr]
    for h in hr:
        c_scr[h] = decay[h] * c_old[h] + _dot_tn(kw[h].astype(BF16), vb[h])
    n_scr[...] = jnp.concatenate([decay[h] * n_old[h] + jnp.sum(kw[h], axis=0, keepdims=True) for h in hr], axis=0)
    m_scr[...] = jnp.concatenate([jnp.broadcast_to(m_new[h], (1, m_scr.shape[1])) for h in hr], axis=0)

    lb = lb_ref[...]
    f = lb + (1.0 - lb) * jax.nn.sigmoid(fd_ref[...])
    b_all = cumsum_rows(jnp.log(f))
    kk_all = 1.0 - f
    qd = qd_ref[...]
    qq_all = qd * jax.nn.sigmoid(qd)
    b_end = b_all[L - 1:L, :]
    q_in = (qq_all * jnp.exp(b_all)).astype(BF16)
    k_out = (kk_all * jnp.exp(b_end - b_all)).astype(BF16)
    s_decay = jnp.exp(b_end)
    vd = [id_ref[:, hsl[h]].astype(BF16) for h in hr]
    st_old = [s_scr[h] for h in hr]
    o_inter = [_dot_nt(q_in[:, hsl[h]], st_old[h].astype(BF16)) for h in hr]
    for h in hr:
        s_scr[h] = st_old[h] * s_decay[:, hsl[h]] + _dot_tn(vd[h], k_out[:, hsl[h]])

    row_hd = lax.broadcasted_iota(jnp.int32, (L, hd_all), 0)
    a_acc = [jnp.zeros((L, L), F32) for _ in hr]
    m = V7X_SUBLANES
    while m < L:
        nb = L // (2 * m)
        anchor = [jnp.broadcast_to(b_all[i * 2 * m + m - 1:i * 2 * m + m, :], (2 * m, hd_all)) for i in range(nb)]
        anchor = jnp.concatenate(anchor, axis=0) if nb > 1 else anchor[0]
        upper = (row_hd & m) != 0
        q_m = (qq_all * jnp.exp(jnp.where(upper, b_all - anchor, -jnp.inf))).astype(BF16)
        k_m = (kk_all * jnp.exp(jnp.where(upper, -jnp.inf, anchor - b_all))).astype(BF16)
        same = (row // (2 * m)) == (col // (2 * m))
        for h in hr:
            a_m = _dot_nt(q_m[:, hsl[h]], k_m[:, hsl[h]])
            a_acc[h] = a_acc[h] + (jnp.where(same, a_m, 0.0) if nb > 1 else a_m)
        m *= 2
    n8 = L // V7X_SUBLANES
    sub3 = lax.broadcasted_iota(jnp.int32, (n8, V7X_SUBLANES, hd_all), 1)
    b3 = b_all.reshape(n8, V7X_SUBLANES, hd_all)
    k3 = kk_all.reshape(n8, V7X_SUBLANES, hd_all)
    for j in range(V7X_SUBLANES):
        if j == 0:
            d_j = qq_all * kk_all
        else:
            b_sh = pltpu.roll(b3, j, axis=1)
            k_sh = pltpu.roll(k3, j, axis=1)
            dec = jnp.exp(jnp.where(sub3 >= j, b3 - b_sh, -jnp.inf))
            d_j = qq_all * (k_sh * dec).reshape(L, hd_all)
        on_diag = col == row - j
        for h in hr:
            a_j = jnp.sum(d_j[:, hsl[h]], axis=-1, keepdims=True)
            a_acc[h] = a_acc[h] + jnp.where(on_diag, a_j, 0.0)

    o_all = [_dot(a_acc[h].astype(BF16), vd[h]) + o_inter[h] for h in hr]
    ms_d = [jnp.mean(o_all[h] * o_all[h], axis=-1, keepdims=True) for h in hr]
    for h in hr:
        on = o_all[h] * lax.rsqrt(ms_d[h] + EPS) * nd_ref[:, hsl[h]]
        y_ref[:, hd_all + h * HEAD_DIM: hd_all + (h + 1) * HEAD_DIM] = (
            jax.nn.sigmoid(gd_ref[:, hsl[h]]) * on).astype(y_ref.dtype)

    @pl.when(step == pl.num_programs(1) - 1)
    def _():
        c1_ref[...] = c_scr[...]
        n1_ref[...] = n_scr[...]
        m1_ref[...] = m_scr[...]
        s1_ref[...] = s_scr[...]


def _odd_mixer(z, gcol, grow3, row_off, n_seq, t_len, blk, bcol, brow, norm_c, norm_d, lb, c0, n0, m0, s0t):
    heads = c0.shape[1]
    hd_all = heads * HEAD_DIM
    nblk = t_len // blk
    assert row_off % blk == 0 and t_len % blk == 0
    blk0 = row_off // blk
    gr = grow3.shape[1]

    def zspec(colblk):
        return pl.BlockSpec((blk, hd_all), lambda s, c: (blk0 + s * nblk + c, colblk))

    def whole(a):
        return pl.BlockSpec(a.shape, lambda s, c: (0,) * a.ndim)

    def per_seq(a):
        return pl.BlockSpec((None,) + a.shape[1:], lambda s, c: (s,) + (0,) * (a.ndim - 1))

    state_shapes = [jax.ShapeDtypeStruct(a.shape, F32) for a in (c0, n0, m0, s0t)]
    return pl.pallas_call(
        functools.partial(_odd_body, heads=heads, blk=blk),
        grid=(n_seq, nblk),
        in_specs=[zspec(i) for i in range(8)] + [
            pl.BlockSpec((blk, V7X_LANES), lambda s, c: (blk0 + s * nblk + c, 0)),
            pl.BlockSpec((None, gr, blk), lambda s, c: (s * nblk + c, 0, 0)),
            whole(bcol), whole(brow), whole(norm_c), whole(norm_d), whole(lb),
            per_seq(c0), per_seq(n0), per_seq(m0), per_seq(s0t)],
        out_specs=[pl.BlockSpec((blk, 2 * hd_all), lambda s, c: (s * nblk + c, 0)),
                   per_seq(c0), per_seq(n0), per_seq(m0), per_seq(s0t)],
        out_shape=[jax.ShapeDtypeStruct((n_seq * t_len, 2 * hd_all), BF16)] + state_shapes,
        scratch_shapes=[pltpu.VMEM(c0.shape[1:], F32), pltpu.VMEM(n0.shape[1:], F32),
                        pltpu.VMEM(m0.shape[1:], F32), pltpu.VMEM(s0t.shape[1:], F32)],
        compiler_params=_params("arbitrary", "arbitrary"),
        name="odd_mixer",
    )(z, z, z, z, z, z, z, z, gcol, grow3, bcol, brow, norm_c, norm_d, lb, c0, n0, m0, s0t)


def kernel(x_prompt, x_sample, state_conv_a, state_conv_b, state_mlstm_c, state_mlstm_n, state_mlstm_m,
           state_hgrn_s, norm_g, norm_f, ffn_w_gate, ffn_w_up, ffn_w_down, even_w_in, even_conv_a,
           even_conv_b, even_conv_b_bias, even_ln_g, even_ln_b, even_w_out, odd_w_in, odd_bias_i,
           odd_bias_f, odd_norm_c, odd_lb_logits, odd_norm_d, odd_w_out):
    bp, tp, d = x_prompt.shape
    bs, ts, _ = x_sample.shape
    mp, ms = bp * tp, bs * ts
    depth = norm_g.shape[0]
    c_a, c_b = even_conv_a.shape[-1], even_conv_b.shape[-1]
    wa_len, wb_len = even_conv_a.shape[1], even_conv_b.shape[1]
    heads = state_mlstm_c.shape[2]
    hd_all = heads * HEAD_DIM
    assert c_a == c_b and even_w_in.shape[-1] == 5 * c_a
    assert state_mlstm_c.shape[3:] == (HEAD_DIM, HEAD_DIM) and state_hgrn_s.shape[2:] == (heads, HEAD_DIM, HEAD_DIM)
    assert odd_w_in.shape[-1] == 8 * hd_all + 2 * heads and 2 * heads <= V7X_LANES

    x = jnp.concatenate([x_prompt.reshape(mp, d), x_sample.reshape(ms, d)], axis=0)

    w_ffn = (ffn_w_gate[0, 0].astype(BF16), ffn_w_up[0, 0].astype(BF16), ffn_w_down[0, 0].astype(BF16))
    ffn_stacks = (ffn_w_gate.astype(F32), ffn_w_up.astype(F32), ffn_w_down.astype(F32))
    ew_in, ew_out = even_w_in.astype(F32), even_w_out.astype(F32)
    ow_in, ow_out = odd_w_in.astype(F32), odd_w_out.astype(F32)
    w_gate_cols = odd_w_in[:, :, 8 * hd_all:]
    gr = -(-2 * heads // V7X_SUBLANES) * V7X_SUBLANES
    wgc = jnp.pad(w_gate_cols, ((0, 0), (0, 0), (0, V7X_LANES - 2 * heads))).astype(BF16)
    wgr = jnp.pad(jnp.swapaxes(w_gate_cols, 1, 2), ((0, 0), (0, gr - 2 * heads), (0, 0))).astype(BF16)
    gate_bias = jnp.concatenate([odd_bias_i, odd_bias_f], axis=-1).astype(F32)
    bcol = jnp.pad(gate_bias, ((0, 0), (0, V7X_LANES - 2 * heads)))[:, None, :]
    brow = jnp.pad(gate_bias, ((0, 0), (0, gr - 2 * heads)))[:, :, None]
    lb_all = _lower_bounds(odd_lb_logits.astype(F32))

    ha = -(-(wa_len - 1) // V7X_SUBLANES) * V7X_SUBLANES
    hb = -(-(wb_len - 1) // V7X_SUBLANES) * V7X_SUBLANES

    def hist(state, n, h_rows, w_len):
        if state is None:
            return jnp.zeros((n, h_rows, c_a), F32)
        return jnp.pad(state.astype(F32), ((0, 0), (h_rows - (w_len - 1), 0), (0, 0)))

    blk_p = _pow2_block(tp, CHUNK_CAP)
    blk_s = _pow2_block(ts, CHUNK_CAP)
    groups = ((0, bp, tp, blk_p), (mp, bs, ts, blk_s))

    new_a, new_b, new_c, new_n, new_m, new_s = ([[], []] for _ in range(6))
    for l in range(depth):
        j = l // 2
        x, w_ffn = _ffn(x, norm_g[l, 0][None, :], *w_ffn, nxt=ffn_stacks + (l, 1))
        if l % 2 == 0:
            z = _proj(x, norm_g[l, 1][None, :], ew_in, j, 5 * c_a)
            ys = []
            for gi, (off, n, t_len, _) in enumerate(groups):
                sa = None if gi == 0 else state_conv_a[j]
                sb = None if gi == 0 else state_conv_b[j]
                y, a1, b1 = _even_mixer(
                    z, off, n, t_len, hist(sa, n, ha, wa_len), hist(sb, n, hb, wb_len),
                    even_conv_a[j].astype(F32), even_conv_b[j].astype(F32),
                    even_conv_b_bias[j][None, :], even_ln_g[j][None, :], even_ln_b[j][None, :])
                ys.append(y)
                new_a[gi].append(a1)
                new_b[gi].append(b1)
            x = _outproj(x, jnp.concatenate(ys, axis=0), ew_out, j)
        else:
            z, gcol, grow = _proj(x, norm_g[l, 1][None, :], ow_in, j, 8 * hd_all, wgc, wgr)
            ys = []
            for gi, (off, n, t_len, blk) in enumerate(groups):
                grow3 = grow[:, off:off + n * t_len].reshape(gr, n * t_len // blk, blk).transpose(1, 0, 2)
                if gi == 0:
                    c0 = jnp.zeros((n, heads, HEAD_DIM, HEAD_DIM), F32)
                    n0 = jnp.zeros((n, heads, HEAD_DIM), F32)
                    m0 = jnp.zeros((n, heads, V7X_LANES), F32)
                    s0t = jnp.zeros((n, heads, HEAD_DIM, HEAD_DIM), F32)
                else:
                    c0 = state_mlstm_c[j].astype(F32)
                    n0 = state_mlstm_n[j].astype(F32)
                    m0 = jnp.broadcast_to(state_mlstm_m[j].astype(F32)[:, :, None], (n, heads, V7X_LANES))
                    s0t = jnp.swapaxes(state_hgrn_s[j].astype(F32), -1, -2)
                y, c1, n1, m1, s1t = _odd_mixer(
                    z, gcol, grow3, off, n, t_len, blk, bcol[j], brow[j], odd_norm_c[j][None, :],
                    odd_norm_d[j][None, :], lb_all[j][None, :], c0, n0, m0, s0t)
                ys.append(y)
                new_c[gi].append(c1)
                new_n[gi].append(n1)
                new_m[gi].append(m1[:, :, 0])
                new_s[gi].append(jnp.swapaxes(s1t, -1, -2))
            x = _outproj(x, jnp.concatenate(ys, axis=0), ow_out, j)
        if l == depth - 1:
            x, _ = _ffn(x, norm_g[l, 2][None, :], *w_ffn, gf=norm_f[None, :])
        else:
            x, w_ffn = _ffn(x, norm_g[l, 2][None, :], *w_ffn, nxt=ffn_stacks + (l + 1, 0))

    y_prompt = x[:mp].reshape(bp, tp, d)
    y_sample = x[mp:].reshape(bs, ts, d)
    st = lambda lst, gi: jnp.stack(lst[gi])
    return (y_prompt, y_sample,
            st(new_a, 0), st(new_b, 0), st(new_c, 0), st(new_n, 0), st(new_m, 0), st(new_s, 0),
            st(new_a, 1), st(new_b, 1), st(new_c, 1), st(new_n, 1), st(new_m, 1), st(new_s, 1))
```

```python
import functools

import jax
import jax.numpy as jnp
from jax import lax
from jax.experimental import pallas as pl
from jax.experimental.pallas import tpu as pltpu

F32 = jnp.float32
BF16 = jnp.bfloat16
EPS = 1e-6

V7X_LANES = 128
V7X_SUBLANES = 8
V7X_VMEM_LIMIT = 56 * 1024 * 1024
HEAD_DIM = 128
ROW_TILE_CAP = 768
FF_TILE_CAP = 512
COL_TILE_CAP = 1024
CONV_TILE_CAP = 256
CHUNK_CAP = 128


def _tile(n, cap, align):
    if n <= cap:
        return n
    for d in range(cap - cap % align, 0, -align):
        if n % d == 0:
            return d
    raise ValueError(f"no tile for {n} (cap {cap}, align {align})")


def _pow2_block(n, cap):
    b = V7X_SUBLANES
    assert n % b == 0
    while 2 * b <= cap and n % (2 * b) == 0:
        b *= 2
    return b


def _params(*sem):
    return pltpu.CompilerParams(dimension_semantics=sem, vmem_limit_bytes=V7X_VMEM_LIMIT)


def _rms(x, g):
    return x * lax.rsqrt(jnp.mean(x * x, axis=-1, keepdims=True) + EPS) * g


def _dot(a, b):
    return jnp.dot(a, b, preferred_element_type=F32)


def _dot_nt(a, b):
    return lax.dot_general(a, b, (((1,), (1,)), ((), ())), preferred_element_type=F32)


def _dot_tn(a, b):
    return lax.dot_general(a, b, (((0,), (0,)), ((), ())), preferred_element_type=F32)


def _split3(x):
    hi = x.astype(BF16)
    r1 = x - hi.astype(F32)
    mid = r1.astype(BF16)
    lo = (r1 - mid.astype(F32)).astype(BF16)
    return hi, mid, lo


def _log_sigmoid(x):
    return jnp.minimum(x, 0.0) - jnp.log1p(jnp.exp(-jnp.abs(x)))


def _ffn_body(x_ref, g_ref, wg_ref, wu_ref, wd_ref, *rest, final_norm, cast_next):
    rest = list(rest)
    gf_ref = rest.pop(0) if final_norm else None
    nxt_in = [rest.pop(0) for _ in range(3)] if cast_next else []
    o_ref = rest.pop(0)
    nxt_out = [rest.pop(0) for _ in range(3)] if cast_next else []
    (xn_ref,) = rest
    j = pl.program_id(1)

    @pl.when(j == 0)
    def _():
        xn_ref[...] = _rms(x_ref[...], g_ref[...]).astype(BF16)
        o_ref[...] = jnp.zeros_like(o_ref)

    xn = xn_ref[...]
    hg = _dot(xn, wg_ref[...])
    hu = _dot(xn, wu_ref[...])
    a = (hg * jax.nn.sigmoid(hg) * hu).astype(BF16)
    o_ref[...] += _dot(a, wd_ref[...])

    @pl.when(j == pl.num_programs(1) - 1)
    def _():
        y = x_ref[...] + 0.5 * o_ref[...]
        if final_norm:
            y = _rms(y, gf_ref[...])
        o_ref[...] = y

    for src, dst in zip(nxt_in, nxt_out):
        dst[...] = src[...].astype(BF16)


def _ffn(x, g, wg, wu, wd, gf=None, nxt=None):
    m, d = x.shape
    f = wg.shape[-1]
    tm = _tile(m, ROW_TILE_CAP, V7X_LANES)
    tf = _tile(f, FF_TILE_CAP, V7X_LANES)
    n_i = m // tm
    in_specs = [
        pl.BlockSpec((tm, d), lambda i, j: (i, 0)),
        pl.BlockSpec((1, d), lambda i, j: (0, 0)),
        pl.BlockSpec((d, tf), lambda i, j: (0, j)),
        pl.BlockSpec((d, tf), lambda i, j: (0, j)),
        pl.BlockSpec((tf, d), lambda i, j: (j, 0)),
    ]
    out_specs = [pl.BlockSpec((tm, d), lambda i, j: (i, 0))]
    out_shape = [jax.ShapeDtypeStruct((m, d), F32)]
    args = [x, g, wg, wu, wd]
    if gf is not None:
        in_specs.append(pl.BlockSpec((1, d), lambda i, j: (0, 0)))
        args.append(gf)
    if nxt is not None:
        ng, nu, nd, ln, kn = nxt
        rb = next(r for r in range(V7X_LANES, d + 1, V7X_LANES) if d % r == 0 and d // r <= n_i)
        last_r, last_c = d // rb - 1, f // tf - 1

        def blk_r(i, j):
            return jnp.minimum(i, last_r)

        def blk_c(i, j):
            return jnp.where(i > last_r, last_c, j)

        in_specs += [
            pl.BlockSpec((None, None, rb, tf), lambda i, j: (ln, kn, blk_r(i, j), blk_c(i, j))),
            pl.BlockSpec((None, None, rb, tf), lambda i, j: (ln, kn, blk_r(i, j), blk_c(i, j))),
            pl.BlockSpec((None, None, tf, rb), lambda i, j: (ln, kn, blk_c(i, j), blk_r(i, j))),
        ]
        out_specs += [
            pl.BlockSpec((rb, tf), lambda i, j: (blk_r(i, j), blk_c(i, j))),
            pl.BlockSpec((rb, tf), lambda i, j: (blk_r(i, j), blk_c(i, j))),
            pl.BlockSpec((tf, rb), lambda i, j: (blk_c(i, j), blk_r(i, j))),
        ]
        out_shape += [jax.ShapeDtypeStruct((d, f), BF16), jax.ShapeDtypeStruct((d, f), BF16),
                      jax.ShapeDtypeStruct((f, d), BF16)]
        args += [ng, nu, nd]
    outs = pl.pallas_call(
        functools.partial(_ffn_body, final_norm=gf is not None, cast_next=nxt is not None),
        grid=(n_i, f // tf),
        in_specs=in_specs,
        out_specs=out_specs,
        out_shape=out_shape,
        scratch_shapes=[pltpu.VMEM((tm, d), BF16)],
        compiler_params=_params("arbitrary", "arbitrary"),
        name="ffn_final" if gf is not None else "ffn",
    )(*args)
    return outs[0], tuple(outs[1:])


def _proj_body(x_ref, g_ref, w_ref, *rest, gates):
    if gates:
        wgc_ref, wgr_ref, z_ref, gc_ref, gr_ref, xn_ref = rest
    else:
        z_ref, xn_ref = rest

    @pl.when(pl.program_id(1) == 0)
    def _():
        xn = _rms(x_ref[...], g_ref[...]).astype(BF16)
        xn_ref[...] = xn
        if gates:
            gc_ref[...] = _dot(xn, wgc_ref[...])
            gr_ref[...] = _dot_nt(wgr_ref[...], xn)

    z_ref[...] = _dot(xn_ref[...], w_ref[...])


def _proj(x, g, w, jl, n, wgc=None, wgr=None):
    m, d = x.shape
    tm = _tile(m, ROW_TILE_CAP, V7X_LANES)
    tn = _tile(n, COL_TILE_CAP, V7X_LANES)
    gates = wgc is not None
    in_specs = [
        pl.BlockSpec((tm, d), lambda i, j: (i, 0)),
        pl.BlockSpec((1, d), lambda i, j: (0, 0)),
        pl.BlockSpec((None, d, tn), lambda i, j: (jl, 0, j)),
    ]
    out_specs = [pl.BlockSpec((tm, tn), lambda i, j: (i, j))]
    out_shape = [jax.ShapeDtypeStruct((m, n), F32)]
    args = [x, g, w]
    if gates:
        gr = wgr.shape[1]
        in_specs += [
            pl.BlockSpec((None, d, V7X_LANES), lambda i, j: (jl, 0, 0)),
            pl.BlockSpec((None, gr, d), lambda i, j: (jl, 0, 0)),
        ]
        out_specs += [
            pl.BlockSpec((tm, V7X_LANES), lambda i, j: (i, 0)),
            pl.BlockSpec((gr, tm), lambda i, j: (0, i)),
        ]
        out_shape += [jax.ShapeDtypeStruct((m, V7X_LANES), F32), jax.ShapeDtypeStruct((gr, m), F32)]
        args += [wgc, wgr]
    outs = pl.pallas_call(
        functools.partial(_proj_body, gates=gates),
        grid=(m // tm, n // tn),
        in_specs=in_specs,
        out_specs=out_specs,
        out_shape=out_shape,
        scratch_shapes=[pltpu.VMEM((tm, d), BF16)],
        compiler_params=_params("parallel", "arbitrary"),
        name="proj_gates" if gates else "proj",
    )(*args)
    return outs if gates else outs[0]


def _outproj_body(x_ref, y_ref, w_ref, o_ref):
    o_ref[...] = x_ref[...] + _dot(y_ref[...], w_ref[...])


def _outproj(x, y, w, jl):
    m, d = x.shape
    kdim = y.shape[1]
    tm = _tile(m, ROW_TILE_CAP, V7X_LANES)
    return pl.pallas_call(
        _outproj_body,
        grid=(m // tm,),
        in_specs=[
            pl.BlockSpec((tm, d), lambda i: (i, 0)),
            pl.BlockSpec((tm, kdim), lambda i: (i, 0)),
            pl.BlockSpec((None, kdim, d), lambda i: (jl, 0, 0), pipeline_mode=pl.Buffered(1)),
        ],
        out_specs=pl.BlockSpec((tm, d), lambda i: (i, 0)),
        out_shape=jax.ShapeDtypeStruct((m, d), F32),
        compiler_params=_params("parallel"),
        name="outproj",
    )(x, y, w)


def _even_body(xa_ref, gb_ref, gc_ref, gv_ref, gg_ref, ha_ref, hb_ref, wa_ref, wb_ref, bias_ref, lng_ref,
               lnb_ref, ybuf_ref, y_ref, na_ref, nb_ref, ua_scr, ub_scr, sh_scr, *, tc, wa_len, wb_len, ha, hb):
    del ybuf_ref
    t = pl.program_id(1)
    c = xa_ref.shape[1]

    @pl.when(t == 0)
    def _():
        ua_scr[0:ha, :] = ha_ref[...]
        ub_scr[0:hb, :] = hb_ref[...]

    @pl.when(t > 0)
    def _():
        ua_scr[0:ha, :] = ua_scr[tc:tc + ha, :]
        ub_scr[0:hb, :] = ub_scr[tc:tc + hb, :]

    ua_scr[ha:ha + tc, :] = gc_ref[...] * xa_ref[...]
    ub_scr[hb:hb + tc, :] = gv_ref[...] * jax.nn.sigmoid(gg_ref[...])

    def conv(w_ref, u_scr, first):
        n_taps = w_ref.shape[0]
        acc = jnp.zeros((tc, c), F32)
        for r in range(V7X_SUBLANES):
            offs = [o for o in range(first, first + n_taps) if o % V7X_SUBLANES == r]
            if not offs:
                continue
            span = max(offs) - r
            if r == 0:
                src = u_scr
            else:
                sh_scr[0:span + tc, :] = u_scr[r:r + span + tc, :]
                src = sh_scr
            for o in offs:
                acc = acc + w_ref[o - first:o - first + 1, :] * src[o - r:o - r + tc, :]
        return acc

    ca = conv(wa_ref, ua_scr, ha - (wa_len - 1))
    y_ref[:, 0:c] = (gb_ref[...] * ca).astype(y_ref.dtype)

    cb = conv(wb_ref, ub_scr, hb - (wb_len - 1))
    cb = cb + bias_ref[...]
    mu = jnp.mean(cb, axis=-1, keepdims=True)
    dc = cb - mu
    var = jnp.mean(dc * dc, axis=-1, keepdims=True)
    ln = dc * lax.rsqrt(var + EPS) * lng_ref[...] + lnb_ref[...]
    y_ref[:, c:2 * c] = (ln * jax.nn.sigmoid(ln)).astype(y_ref.dtype)

    @pl.when(t == pl.num_programs(1) - 1)
    def _():
        na_ref[...] = ua_scr[ha + tc - (wa_len - 1): ha + tc, :]
        nb_ref[...] = ub_scr[hb + tc - (wb_len - 1): hb + tc, :]


def _even_mixer(z, row_off, n_seq, t_len, hist_a, hist_b, conv_a, conv_b, bias, ln_g, ln_b, ybuf):
    c = conv_a.shape[-1]
    wa_len, wb_len = conv_a.shape[0], conv_b.shape[0]
    ha, hb = hist_a.shape[1], hist_b.shape[1]
    tc = _tile(t_len, CONV_TILE_CAP, V7X_SUBLANES)
    nt = t_len // tc
    assert row_off % tc == 0 and tc >= hb and tc >= ha
    blk0 = row_off // tc

    def zspec(col):
        return pl.BlockSpec((tc, c), lambda s, t: (blk0 + s * nt + t, col))

    def whole(a):
        return pl.BlockSpec(a.shape, lambda s, t: (0,) * a.ndim)

    return pl.pallas_call(
        functools.partial(_even_body, tc=tc, wa_len=wa_len, wb_len=wb_len, ha=ha, hb=hb),
        grid=(n_seq, nt),
        in_specs=[zspec(0), zspec(1), zspec(2), zspec(3), zspec(4),
                  pl.BlockSpec((None, ha, c), lambda s, t: (s, 0, 0)),
                  pl.BlockSpec((None, hb, c), lambda s, t: (s, 0, 0)),
                  whole(conv_a), whole(conv_b), whole(bias), whole(ln_g), whole(ln_b),
                  pl.BlockSpec(memory_space=pl.ANY)],
        out_specs=[pl.BlockSpec((tc, 2 * c), lambda s, t: (blk0 + s * nt + t, 0)),
                   pl.BlockSpec((None, wa_len - 1, c), lambda s, t: (s, 0, 0)),
                   pl.BlockSpec((None, wb_len - 1, c), lambda s, t: (s, 0, 0))],
        out_shape=[jax.ShapeDtypeStruct(ybuf.shape, ybuf.dtype),
                   jax.ShapeDtypeStruct((n_seq, wa_len - 1, c), F32),
                   jax.ShapeDtypeStruct((n_seq, wb_len - 1, c), F32)],
        scratch_shapes=[pltpu.VMEM((ha + tc, c), F32), pltpu.VMEM((hb + tc, c), F32),
                        pltpu.VMEM((max(ha, hb) + tc, c), F32)],
        compiler_params=_params("arbitrary", "arbitrary"),
        input_output_aliases={12: 0},
        name="even_mixer",
    )(z, z, z, z, z, hist_a, hist_b, conv_a, conv_b, bias, ln_g, ln_b, ybuf)


def _lb_body(logit_ref, lb_ref):
    n = logit_ref.shape[0]
    rows = [logit_ref[j:j + 1, :] for j in range(n)]
    mx = functools.reduce(jnp.maximum, rows)
    es = [jnp.exp(r - mx) for r in rows]
    tot = functools.reduce(lambda a, b: a + b, es)
    first = es[0] / tot
    run = jnp.zeros_like(first)
    for j in range(n):
        run = run + es[j] / tot
        lb_ref[j:j + 1, :] = run - first


def _lower_bounds(logits):
    return pl.pallas_call(
        _lb_body,
        out_shape=jax.ShapeDtypeStruct(logits.shape, F32),
        name="hgrn_lower_bounds",
    )(logits)


def _odd_body(qc_ref, kc_ref, vc_ref, oc_ref, qd_ref, fd_ref, id_ref, gd_ref, gcol_ref, grow_ref,
              bcol_ref, brow_ref, nc_ref, nd_ref, lb_ref, c0_ref, n0_ref, m0_ref, s0_ref, ybuf_ref,
              y_ref, c1_ref, n1_ref, m1_ref, s1_ref, c_scr, n_scr, m_scr, s_scr, *, heads, blk):
    del ybuf_ref
    L = blk
    hd_all = heads * HEAD_DIM
    step = pl.program_id(1)

    @pl.when(step == 0)
    def _():
        c_scr[...] = c0_ref[...]
        n_scr[...] = n0_ref[...]
        m_scr[...] = m0_ref[...]
        s_scr[...] = s0_ref[...]

    row = lax.broadcasted_iota(jnp.int32, (L, L), 0)
    col = lax.broadcasted_iota(jnp.int32, (L, L), 1)
    causal = row >= col
    tri_l = causal.astype(BF16)
    tri_u = (row <= col).astype(BF16)

    def cumsum_rows(x):
        hi, mid, lo = _split3(x)
        return _dot(tri_l, hi) + _dot(tri_l, mid) + _dot(tri_l, lo)

    def cumsum_lanes(x):
        hi, mid, lo = _split3(x)
        return _dot(hi, tri_u) + _dot(mid, tri_u) + _dot(lo, tri_u)

    gcol = gcol_ref[...] + bcol_ref[...]
    grow = grow_ref[...] + brow_ref[...]
    bt_col = cumsum_rows(_log_sigmoid(gcol))
    bt_row = cumsum_lanes(_log_sigmoid(grow))
    scale = HEAD_DIM ** -0.5
    m_all = m_scr[...]
    n_all = n_scr[...]
    hr = range(heads)
    hsl = [slice(h * HEAD_DIM, (h + 1) * HEAD_DIM) for h in hr]
    b_c = [bt_col[:, heads + h:heads + h + 1] for h in hr]
    i_c = [gcol[:, h:h + 1] for h in hr]
    b_r = [bt_row[heads + h:heads + h + 1, :] for h in hr]
    i_r = [grow[h:h + 1, :] for h in hr]
    m_prev = [m_all[h:h + 1, 0:1] for h in hr]
    dmat = [jnp.where(causal, b_c[h] - b_r[h] + i_r[h], -jnp.inf) for h in hr]
    inter = [b_c[h] + m_prev[h] for h in hr]
    m_row = [jnp.maximum(inter[h], jnp.max(dmat[h], axis=-1, keepdims=True)) for h in hr]
    q = [qc_ref[:, hsl[h]] for h in hr]
    k = [kc_ref[:, hsl[h]] * scale for h in hr]
    qb = [q[h].astype(BF16) for h in hr]
    vb = [vc_ref[:, hsl[h]].astype(BF16) for h in hr]
    s_mat = [_dot_nt(qb[h], k[h].astype(BF16)) * jnp.exp(dmat[h] - m_row[h]) for h in hr]
    w_inter = [jnp.exp(inter[h] - m_row[h]) for h in hr]
    n_old = [n_all[h:h + 1, :] for h in hr]
    den = [jnp.sum(s_mat[h], axis=-1, keepdims=True)
           + w_inter[h] * jnp.sum(q[h] * n_old[h], axis=-1, keepdims=True) for h in hr]
    c_old = [c_scr[h] for h in hr]
    num = [_dot(s_mat[h].astype(BF16), vb[h]) + w_inter[h] * _dot(qb[h], c_old[h].astype(BF16)) for h in hr]
    hh = [num[h] / jnp.maximum(jnp.abs(den[h]), jnp.exp(-m_row[h])) for h in hr]
    ms = [jnp.mean(hh[h] * hh[h], axis=-1, keepdims=True) for h in hr]
    for h in hr:
        hn = hh[h] * lax.rsqrt(ms[h] + EPS) * nc_ref[:, hsl[h]]
        y_ref[:, hsl[h]] = (jax.nn.sigmoid(oc_ref[:, hsl[h]]) * hn).astype(y_ref.dtype)
    b_last = [b_r[h][:, L - 1:L] for h in hr]
    m_new = [jnp.maximum(b_last[h] + m_prev[h], jnp.max(b_last[h] - b_r[h] + i_r[h], axis=-1, keepdims=True))
             for h in hr]
    kw = [k[h] * jnp.exp(b_last[h] - b_c[h] + i_c[h] - m_new[h]) for h in hr]
    decay = [jnp.exp(b_last[h] + m_prev[h] - m_new[h]) for h in hr]
    for h in hr:
        c_scr[h] = decay[h] * c_old[h] + _dot_tn(kw[h].astype(BF16), vb[h])
    n_scr[...] = jnp.concatenate([decay[h] * n_old[h] + jnp.sum(kw[h], axis=0, keepdims=True) for h in hr], axis=0)
    m_scr[...] = jnp.concatenate([jnp.broadcast_to(m_new[h], (1, m_scr.shape[1])) for h in hr], axis=0)

    lb = lb_ref[...]
    f = lb + (1.0 - lb) * jax.nn.sigmoid(fd_ref[...])
    b_all = cumsum_rows(jnp.log(f))
    kk_all = 1.0 - f
    qd = qd_ref[...]
    qq_all = qd * jax.nn.sigmoid(qd)
    b_end = b_all[L - 1:L, :]
    q_in = (qq_all * jnp.exp(b_all)).astype(BF16)
    k_out = (kk_all * jnp.exp(b_end - b_all)).astype(BF16)
    s_decay = jnp.exp(b_end)
    vd = [id_ref[:, hsl[h]].astype(BF16) for h in hr]
    st_old = [s_scr[h] for h in hr]
    o_inter = [_dot_nt(q_in[:, hsl[h]], st_old[h].astype(BF16)) for h in hr]
    for h in hr:
        s_scr[h] = st_old[h] * s_decay[:, hsl[h]] + _dot_tn(vd[h], k_out[:, hsl[h]])

    row_hd = lax.broadcasted_iota(jnp.int32, (L, hd_all), 0)
    a_acc = [jnp.zeros((L, L), F32) for _ in hr]
    m = V7X_SUBLANES
    while m < L:
        nb = L // (2 * m)
        anchor = [jnp.broadcast_to(b_all[i * 2 * m + m - 1:i * 2 * m + m, :], (2 * m, hd_all)) for i in range(nb)]
        anchor = jnp.concatenate(anchor, axis=0) if nb > 1 else anchor[0]
        upper = (row_hd & m) != 0
        q_m = (qq_all * jnp.exp(jnp.where(upper, b_all - anchor, -jnp.inf))).astype(BF16)
        k_m = (kk_all * jnp.exp(jnp.where(upper, -jnp.inf, anchor - b_all))).astype(BF16)
        same = (row // (2 * m)) == (col // (2 * m))
        for h in hr:
            a_m = _dot_nt(q_m[:, hsl[h]], k_m[:, hsl[h]])
            a_acc[h] = a_acc[h] + (jnp.where(same, a_m, 0.0) if nb > 1 else a_m)
        m *= 2
    n8 = L // V7X_SUBLANES
    sub3 = lax.broadcasted_iota(jnp.int32, (n8, V7X_SUBLANES, hd_all), 1)
    b3 = b_all.reshape(n8, V7X_SUBLANES, hd_all)
    m = V7X_SUBLANES // 2
    while m >= 1:
        anchor = None
        for i in range(V7X_SUBLANES // (2 * m)):
            piece = jnp.broadcast_to(b3[:, i * 2 * m + m - 1:i * 2 * m + m, :], b3.shape)
            anchor = piece if anchor is None else jnp.where(sub3 >= i * 2 * m, piece, anchor)
        anchor = anchor.reshape(L, hd_all)
        upper = (row_hd & m) != 0
        q_m = (qq_all * jnp.exp(jnp.where(upper, b_all - anchor, -jnp.inf))).astype(BF16)
        k_m = (kk_all * jnp.exp(jnp.where(upper, -jnp.inf, anchor - b_all))).astype(BF16)
        same = (row // (2 * m)) == (col // (2 * m))
        for h in hr:
            a_acc[h] = a_acc[h] + jnp.where(same, _dot_nt(q_m[:, hsl[h]], k_m[:, hsl[h]]), 0.0)
        m //= 2
    q_0, k_0 = qq_all.astype(BF16), kk_all.astype(BF16)
    for h in hr:
        a_acc[h] = a_acc[h] + jnp.where(row == col, _dot_nt(q_0[:, hsl[h]], k_0[:, hsl[h]]), 0.0)

    o_all = [_dot(a_acc[h].astype(BF16), vd[h]) + o_inter[h] for h in hr]
    ms_d = [jnp.mean(o_all[h] * o_all[h], axis=-1, keepdims=True) for h in hr]
    for h in hr:
        on = o_all[h] * lax.rsqrt(ms_d[h] + EPS) * nd_ref[:, hsl[h]]
        y_ref[:, hd_all + h * HEAD_DIM: hd_all + (h + 1) * HEAD_DIM] = (
            jax.nn.sigmoid(gd_ref[:, hsl[h]]) * on).astype(y_ref.dtype)

    @pl.when(step == pl.num_programs(1) - 1)
    def _():
        c1_ref[...] = c_scr[...]
        n1_ref[...] = n_scr[...]
        m1_ref[...] = m_scr[...]
        s1_ref[...] = s_scr[...]


def _odd_mixer(z, gcol, grow3, row_off, n_seq, t_len, blk, bcol, brow, norm_c, norm_d, lb, c0, n0, m0, s0t, ybuf):
    heads = c0.shape[1]
    hd_all = heads * HEAD_DIM
    nblk = t_len // blk
    assert row_off % blk == 0 and t_len % blk == 0
    blk0 = row_off // blk
    gr = grow3.shape[1]

    def zspec(colblk):
        return pl.BlockSpec((blk, hd_all), lambda s, c: (blk0 + s * nblk + c, colblk))

    def whole(a):
        return pl.BlockSpec(a.shape, lambda s, c: (0,) * a.ndim)

    def per_seq(a):
        return pl.BlockSpec((None,) + a.shape[1:], lambda s, c: (s,) + (0,) * (a.ndim - 1))

    state_shapes = [jax.ShapeDtypeStruct(a.shape, F32) for a in (c0, n0, m0, s0t)]
    return pl.pallas_call(
        functools.partial(_odd_body, heads=heads, blk=blk),
        grid=(n_seq, nblk),
        in_specs=[zspec(i) for i in range(8)] + [
            pl.BlockSpec((blk, V7X_LANES), lambda s, c: (blk0 + s * nblk + c, 0)),
            pl.BlockSpec((None, gr, blk), lambda s, c: (s * nblk + c, 0, 0)),
            whole(bcol), whole(brow), whole(norm_c), whole(norm_d), whole(lb),
            per_seq(c0), per_seq(n0), per_seq(m0), per_seq(s0t), pl.BlockSpec(memory_space=pl.ANY)],
        out_specs=[pl.BlockSpec((blk, 2 * hd_all), lambda s, c: (blk0 + s * nblk + c, 0)),
                   per_seq(c0), per_seq(n0), per_seq(m0), per_seq(s0t)],
        out_shape=[jax.ShapeDtypeStruct(ybuf.shape, ybuf.dtype)] + state_shapes,
        scratch_shapes=[pltpu.VMEM(c0.shape[1:], F32), pltpu.VMEM(n0.shape[1:], F32),
                        pltpu.VMEM(m0.shape[1:], F32), pltpu.VMEM(s0t.shape[1:], F32)],
        compiler_params=_params("arbitrary", "arbitrary"),
        input_output_aliases={19: 0},
        name="odd_mixer",
    )(z, z, z, z, z, z, z, z, gcol, grow3, bcol, brow, norm_c, norm_d, lb, c0, n0, m0, s0t, ybuf)


def kernel(x_prompt, x_sample, state_conv_a, state_conv_b, state_mlstm_c, state_mlstm_n, state_mlstm_m,
           state_hgrn_s, norm_g, norm_f, ffn_w_gate, ffn_w_up, ffn_w_down, even_w_in, even_conv_a,
           even_conv_b, even_conv_b_bias, even_ln_g, even_ln_b, even_w_out, odd_w_in, odd_bias_i,
           odd_bias_f, odd_norm_c, odd_lb_logits, odd_norm_d, odd_w_out):
    bp, tp, d = x_prompt.shape
    bs, ts, _ = x_sample.shape
    mp, ms = bp * tp, bs * ts
    depth = norm_g.shape[0]
    c_a, c_b = even_conv_a.shape[-1], even_conv_b.shape[-1]
    wa_len, wb_len = even_conv_a.shape[1], even_conv_b.shape[1]
    heads = state_mlstm_c.shape[2]
    hd_all = heads * HEAD_DIM
    assert c_a == c_b and even_w_in.shape[-1] == 5 * c_a
    assert state_mlstm_c.shape[3:] == (HEAD_DIM, HEAD_DIM) and state_hgrn_s.shape[2:] == (heads, HEAD_DIM, HEAD_DIM)
    assert odd_w_in.shape[-1] == 8 * hd_all + 2 * heads and 2 * heads <= V7X_LANES

    x = jnp.concatenate([x_prompt.reshape(mp, d), x_sample.reshape(ms, d)], axis=0)

    w_ffn = (ffn_w_gate[0, 0].astype(BF16), ffn_w_up[0, 0].astype(BF16), ffn_w_down[0, 0].astype(BF16))
    ffn_stacks = (ffn_w_gate.astype(F32), ffn_w_up.astype(F32), ffn_w_down.astype(F32))
    ew_in, ew_out = even_w_in.astype(BF16), even_w_out.astype(BF16)
    ow_in, ow_out = odd_w_in[:, :, :8 * hd_all].astype(BF16), odd_w_out.astype(BF16)
    w_gate_cols = odd_w_in[:, :, 8 * hd_all:]
    gr = -(-2 * heads // V7X_SUBLANES) * V7X_SUBLANES
    wgc = jnp.pad(w_gate_cols, ((0, 0), (0, 0), (0, V7X_LANES - 2 * heads))).astype(BF16)
    wgr = jnp.pad(jnp.swapaxes(w_gate_cols, 1, 2), ((0, 0), (0, gr - 2 * heads), (0, 0))).astype(BF16)
    gate_bias = jnp.concatenate([odd_bias_i, odd_bias_f], axis=-1).astype(F32)
    bcol = jnp.pad(gate_bias, ((0, 0), (0, V7X_LANES - 2 * heads)))[:, None, :]
    brow = jnp.pad(gate_bias, ((0, 0), (0, gr - 2 * heads)))[:, :, None]
    lb_all = _lower_bounds(odd_lb_logits.astype(F32))

    ha = -(-(wa_len - 1) // V7X_SUBLANES) * V7X_SUBLANES
    hb = -(-(wb_len - 1) // V7X_SUBLANES) * V7X_SUBLANES

    def hist(state, n, h_rows, w_len):
        if state is None:
            return jnp.zeros((n, h_rows, c_a), F32)
        return jnp.pad(state.astype(F32), ((0, 0), (h_rows - (w_len - 1), 0), (0, 0)))

    blk_p = _pow2_block(tp, CHUNK_CAP)
    blk_s = _pow2_block(ts, CHUNK_CAP)
    groups = ((0, bp, tp, blk_p), (mp, bs, ts, blk_s))

    new_a, new_b, new_c, new_n, new_m, new_s = ([[], []] for _ in range(6))
    for l in range(depth):
        j = l // 2
        x, w_ffn = _ffn(x, norm_g[l, 0][None, :], *w_ffn, nxt=ffn_stacks + (l, 1))
        if l % 2 == 0:
            z = _proj(x, norm_g[l, 1][None, :], ew_in, j, 5 * c_a)
            y = jnp.zeros((mp + ms, 2 * c_a), BF16)
            for gi, (off, n, t_len, _) in enumerate(groups):
                sa = None if gi == 0 else state_conv_a[j]
                sb = None if gi == 0 else state_conv_b[j]
                y, a1, b1 = _even_mixer(
                    z, off, n, t_len, hist(sa, n, ha, wa_len), hist(sb, n, hb, wb_len),
                    even_conv_a[j].astype(F32), even_conv_b[j].astype(F32),
                    even_conv_b_bias[j][None, :], even_ln_g[j][None, :], even_ln_b[j][None, :], y)
                new_a[gi].append(a1)
                new_b[gi].append(b1)
            x = _outproj(x, y, ew_out, j)
        else:
            z, gcol, grow = _proj(x, norm_g[l, 1][None, :], ow_in, j, 8 * hd_all, wgc, wgr)
            y = jnp.zeros((mp + ms, 2 * hd_all), BF16)
            for gi, (off, n, t_len, blk) in enumerate(groups):
                grow3 = grow[:, off:off + n * t_len].reshape(gr, n * t_len // blk, blk).transpose(1, 0, 2)
                if gi == 0:
                    c0 = jnp.zeros((n, heads, HEAD_DIM, HEAD_DIM), F32)
                    n0 = jnp.zeros((n, heads, HEAD_DIM), F32)
                    m0 = jnp.zeros((n, heads, V7X_LANES), F32)
                    s0t = jnp.zeros((n, heads, HEAD_DIM, HEAD_DIM), F32)
                else:
                    c0 = state_mlstm_c[j].astype(F32)
                    n0 = state_mlstm_n[j].astype(F32)
                    m0 = jnp.broadcast_to(state_mlstm_m[j].astype(F32)[:, :, None], (n, heads, V7X_LANES))
                    s0t = jnp.swapaxes(state_hgrn_s[j].astype(F32), -1, -2)
                y, c1, n1, m1, s1t = _odd_mixer(
                    z, gcol, grow3, off, n, t_len, blk, bcol[j], brow[j], odd_norm_c[j][None, :],
                    odd_norm_d[j][None, :], lb_all[j][None, :], c0, n0, m0, s0t, y)
                new_c[gi].append(c1)
                new_n[gi].append(n1)
                new_m[gi].append(m1[:, :, 0])
                new_s[gi].append(jnp.swapaxes(s1t, -1, -2))
            x = _outproj(x, y, ow_out, j)
        if l == depth - 1:
            x, _ = _ffn(x, norm_g[l, 2][None, :], *w_ffn, gf=norm_f[None, :])
        else:
            x, w_ffn = _ffn(x, norm_g[l, 2][None, :], *w_ffn, nxt=ffn_stacks + (l + 1, 0))

    y_prompt = x[:mp].reshape(bp, tp, d)
    y_sample = x[mp:].reshape(bs, ts, d)
    st = lambda lst, gi: jnp.stack(lst[gi])
    return (y_prompt, y_sample,
            st(new_a, 0), st(new_b, 0), st(new_c, 0), st(new_n, 0), st(new_m, 0), st(new_s, 0),
            st(new_a, 1), st(new_b, 1), st(new_c, 1), st(new_n, 1), st(new_m, 1), st(new_s, 1))
```

```python
import functools

import jax
import jax.numpy as jnp
from jax import lax
from jax.experimental import pallas as pl
from jax.experimental.pallas import tpu as pltpu

F32 = jnp.float32
BF16 = jnp.bfloat16
EPS = 1e-6

V7X_LANES = 128
V7X_SUBLANES = 8
V7X_VMEM_LIMIT = 56 * 1024 * 1024
HEAD_DIM = 128
ROW_TILE_CAP = 768
FF_TILE_CAP = 512
COL_TILE_CAP = 1024
CONV_TILE_CAP = 256
CHUNK_CAP = 128


def _tile(n, cap, align):
    if n <= cap:
        return n
    for d in range(cap - cap % align, 0, -align):
        if n % d == 0:
            return d
    raise ValueError(f"no tile for {n} (cap {cap}, align {align})")


def _pow2_block(n, cap):
    b = V7X_SUBLANES
    assert n % b == 0
    while 2 * b <= cap and n % (2 * b) == 0:
        b *= 2
    return b


def _params(*sem):
    return pltpu.CompilerParams(dimension_semantics=sem, vmem_limit_bytes=V7X_VMEM_LIMIT)


def _rms(x, g):
    return x * lax.rsqrt(jnp.mean(x * x, axis=-1, keepdims=True) + EPS) * g


def _dot(a, b):
    return jnp.dot(a, b, preferred_element_type=F32)


def _dot_nt(a, b):
    return lax.dot_general(a, b, (((1,), (1,)), ((), ())), preferred_element_type=F32)


def _dot_tn(a, b):
    return lax.dot_general(a, b, (((0,), (0,)), ((), ())), preferred_element_type=F32)


def _split3(x):
    hi = x.astype(BF16)
    r1 = x - hi.astype(F32)
    mid = r1.astype(BF16)
    lo = (r1 - mid.astype(F32)).astype(BF16)
    return hi, mid, lo


def _log_sigmoid(x):
    return jnp.minimum(x, 0.0) - jnp.log1p(jnp.exp(-jnp.abs(x)))


def _ffn_body(x_ref, g_ref, wg_ref, wu_ref, wd_ref, *rest, final_norm, cast_next):
    rest = list(rest)
    gf_ref = rest.pop(0) if final_norm else None
    nxt_in = [rest.pop(0) for _ in range(3)] if cast_next else []
    o_ref = rest.pop(0)
    nxt_out = [rest.pop(0) for _ in range(3)] if cast_next else []
    (xn_ref,) = rest
    j = pl.program_id(1)

    @pl.when(j == 0)
    def _():
        xn_ref[...] = _rms(x_ref[...], g_ref[...]).astype(BF16)
        o_ref[...] = jnp.zeros_like(o_ref)

    xn = xn_ref[...]
    hg = _dot(xn, wg_ref[...])
    hu = _dot(xn, wu_ref[...])
    a = (hg * jax.nn.sigmoid(hg) * hu).astype(BF16)
    o_ref[...] += _dot(a, wd_ref[...])

    @pl.when(j == pl.num_programs(1) - 1)
    def _():
        y = x_ref[...] + 0.5 * o_ref[...]
        if final_norm:
            y = _rms(y, gf_ref[...])
        o_ref[...] = y

    for src, dst in zip(nxt_in, nxt_out):
        dst[...] = src[...].astype(BF16)


def _ffn(x, g, wg, wu, wd, gf=None, nxt=None):
    m, d = x.shape
    f = wg.shape[-1]
    tm = _tile(m, ROW_TILE_CAP, V7X_LANES)
    tf = _tile(f, FF_TILE_CAP, V7X_LANES)
    n_i = m // tm
    in_specs = [
        pl.BlockSpec((tm, d), lambda i, j: (i, 0)),
        pl.BlockSpec((1, d), lambda i, j: (0, 0)),
        pl.BlockSpec((d, tf), lambda i, j: (0, j)),
        pl.BlockSpec((d, tf), lambda i, j: (0, j)),
        pl.BlockSpec((tf, d), lambda i, j: (j, 0)),
    ]
    out_specs = [pl.BlockSpec((tm, d), lambda i, j: (i, 0))]
    out_shape = [jax.ShapeDtypeStruct((m, d), F32)]
    args = [x, g, wg, wu, wd]
    if gf is not None:
        in_specs.append(pl.BlockSpec((1, d), lambda i, j: (0, 0)))
        args.append(gf)
    if nxt is not None:
        ng, nu, nd, ln, kn = nxt
        rb = next(r for r in range(V7X_LANES, d + 1, V7X_LANES) if d % r == 0 and d // r <= n_i)
        last_r, last_c = d // rb - 1, f // tf - 1

        def blk_r(i, j):
            return jnp.minimum(i, last_r)

        def blk_c(i, j):
            return jnp.where(i > last_r, last_c, j)

        in_specs += [
            pl.BlockSpec((None, None, rb, tf), lambda i, j: (ln, kn, blk_r(i, j), blk_c(i, j))),
            pl.BlockSpec((None, None, rb, tf), lambda i, j: (ln, kn, blk_r(i, j), blk_c(i, j))),
            pl.BlockSpec((None, None, tf, rb), lambda i, j: (ln, kn, blk_c(i, j), blk_r(i, j))),
        ]
        out_specs += [
            pl.BlockSpec((rb, tf), lambda i, j: (blk_r(i, j), blk_c(i, j))),
            pl.BlockSpec((rb, tf), lambda i, j: (blk_r(i, j), blk_c(i, j))),
            pl.BlockSpec((tf, rb), lambda i, j: (blk_c(i, j), blk_r(i, j))),
        ]
        out_shape += [jax.ShapeDtypeStruct((d, f), BF16), jax.ShapeDtypeStruct((d, f), BF16),
                      jax.ShapeDtypeStruct((f, d), BF16)]
        args += [ng, nu, nd]
    outs = pl.pallas_call(
        functools.partial(_ffn_body, final_norm=gf is not None, cast_next=nxt is not None),
        grid=(n_i, f // tf),
        in_specs=in_specs,
        out_specs=out_specs,
        out_shape=out_shape,
        scratch_shapes=[pltpu.VMEM((tm, d), BF16)],
        compiler_params=_params("arbitrary", "arbitrary"),
        name="ffn_final" if gf is not None else "ffn",
    )(*args)
    return outs[0], tuple(outs[1:])


def _proj_body(x_ref, g_ref, w_ref, *rest, gates):
    if gates:
        wgc_ref, wgr_ref, z_ref, gc_ref, gr_ref, xn_ref = rest
    else:
        z_ref, xn_ref = rest

    @pl.when(pl.program_id(1) == 0)
    def _():
        xn = _rms(x_ref[...], g_ref[...]).astype(BF16)
        xn_ref[...] = xn
        if gates:
            gc_ref[...] = _dot(xn, wgc_ref[...])
            gr_ref[...] = _dot_nt(wgr_ref[...], xn)

    z_ref[...] = _dot(xn_ref[...], w_ref[...])


def _proj(x, g, w, jl, n, wgc=None, wgr=None):
    m, d = x.shape
    tm = _tile(m, ROW_TILE_CAP, V7X_LANES)
    tn = _tile(n, COL_TILE_CAP, V7X_LANES)
    gates = wgc is not None
    in_specs = [
        pl.BlockSpec((tm, d), lambda i, j: (i, 0)),
        pl.BlockSpec((1, d), lambda i, j: (0, 0)),
        pl.BlockSpec((None, d, tn), lambda i, j: (jl, 0, j)),
    ]
    out_specs = [pl.BlockSpec((tm, tn), lambda i, j: (i, j))]
    out_shape = [jax.ShapeDtypeStruct((m, n), F32)]
    args = [x, g, w]
    if gates:
        gr = wgr.shape[1]
        in_specs += [
            pl.BlockSpec((None, d, V7X_LANES), lambda i, j: (jl, 0, 0)),
            pl.BlockSpec((None, gr, d), lambda i, j: (jl, 0, 0)),
        ]
        out_specs += [
            pl.BlockSpec((tm, V7X_LANES), lambda i, j: (i, 0)),
            pl.BlockSpec((gr, tm), lambda i, j: (0, i)),
        ]
        out_shape += [jax.ShapeDtypeStruct((m, V7X_LANES), F32), jax.ShapeDtypeStruct((gr, m), F32)]
        args += [wgc, wgr]
    outs = pl.pallas_call(
        functools.partial(_proj_body, gates=gates),
        grid=(m // tm, n // tn),
        in_specs=in_specs,
        out_specs=out_specs,
        out_shape=out_shape,
        scratch_shapes=[pltpu.VMEM((tm, d), BF16)],
        compiler_params=_params("parallel", "arbitrary"),
        name="proj_gates" if gates else "proj",
    )(*args)
    return outs if gates else outs[0]


def _outproj_body(x_ref, y_ref, w_ref, o_ref):
    o_ref[...] = x_ref[...] + _dot(y_ref[...], w_ref[...])


def _outproj(x, y, w, jl):
    m, d = x.shape
    kdim = y.shape[1]
    tm = _tile(m, ROW_TILE_CAP, V7X_LANES)
    return pl.pallas_call(
        _outproj_body,
        grid=(m // tm,),
        in_specs=[
            pl.BlockSpec((tm, d), lambda i: (i, 0)),
            pl.BlockSpec((tm, kdim), lambda i: (i, 0)),
            pl.BlockSpec((None, kdim, d), lambda i: (jl, 0, 0), pipeline_mode=pl.Buffered(1)),
        ],
        out_specs=pl.BlockSpec((tm, d), lambda i: (i, 0)),
        out_shape=jax.ShapeDtypeStruct((m, d), F32),
        compiler_params=_params("parallel"),
        name="outproj",
    )(x, y, w)


def _even_body(xa_ref, gb_ref, gc_ref, gv_ref, gg_ref, ha_ref, hb_ref, wa_ref, wb_ref, bias_ref, lng_ref,
               lnb_ref, ybuf_ref, y_ref, na_ref, nb_ref, ua_scr, ub_scr, sh_scr, *, tc, wa_len, wb_len, ha, hb):
    del ybuf_ref
    t = pl.program_id(1)
    c = xa_ref.shape[1]

    @pl.when(t == 0)
    def _():
        ua_scr[0:ha, :] = ha_ref[...]
        ub_scr[0:hb, :] = hb_ref[...]

    @pl.when(t > 0)
    def _():
        ua_scr[0:ha, :] = ua_scr[tc:tc + ha, :]
        ub_scr[0:hb, :] = ub_scr[tc:tc + hb, :]

    ua_scr[ha:ha + tc, :] = gc_ref[...] * xa_ref[...]
    ub_scr[hb:hb + tc, :] = gv_ref[...] * jax.nn.sigmoid(gg_ref[...])

    def conv(w_ref, u_scr, first):
        n_taps = w_ref.shape[0]
        acc = jnp.zeros((tc, c), F32)
        for r in range(V7X_SUBLANES):
            offs = [o for o in range(first, first + n_taps) if o % V7X_SUBLANES == r]
            if not offs:
                continue
            span = max(offs) - r
            if r == 0:
                src = u_scr
            else:
                sh_scr[0:span + tc, :] = u_scr[r:r + span + tc, :]
                src = sh_scr
            for o in offs:
                acc = acc + w_ref[o - first:o - first + 1, :] * src[o - r:o - r + tc, :]
        return acc

    ca = conv(wa_ref, ua_scr, ha - (wa_len - 1))
    y_ref[:, 0:c] = (gb_ref[...] * ca).astype(y_ref.dtype)

    cb = conv(wb_ref, ub_scr, hb - (wb_len - 1))
    cb = cb + bias_ref[...]
    mu = jnp.mean(cb, axis=-1, keepdims=True)
    dc = cb - mu
    var = jnp.mean(dc * dc, axis=-1, keepdims=True)
    ln = dc * lax.rsqrt(var + EPS) * lng_ref[...] + lnb_ref[...]
    y_ref[:, c:2 * c] = (ln * jax.nn.sigmoid(ln)).astype(y_ref.dtype)

    @pl.when(t == pl.num_programs(1) - 1)
    def _():
        na_ref[...] = ua_scr[ha + tc - (wa_len - 1): ha + tc, :]
        nb_ref[...] = ub_scr[hb + tc - (wb_len - 1): hb + tc, :]


def _even_mixer(z, row_off, n_seq, t_len, hist_a, hist_b, conv_a, conv_b, bias, ln_g, ln_b, ybuf):
    c = conv_a.shape[-1]
    wa_len, wb_len = conv_a.shape[0], conv_b.shape[0]
    ha, hb = hist_a.shape[1], hist_b.shape[1]
    tc = _tile(t_len, CONV_TILE_CAP, V7X_SUBLANES)
    nt = t_len // tc
    assert row_off % tc == 0 and tc >= hb and tc >= ha
    blk0 = row_off // tc

    def zspec(col):
        return pl.BlockSpec((tc, c), lambda s, t: (blk0 + s * nt + t, col))

    def whole(a):
        return pl.BlockSpec(a.shape, lambda s, t: (0,) * a.ndim)

    return pl.pallas_call(
        functools.partial(_even_body, tc=tc, wa_len=wa_len, wb_len=wb_len, ha=ha, hb=hb),
        grid=(n_seq, nt),
        in_specs=[zspec(0), zspec(1), zspec(2), zspec(3), zspec(4),
                  pl.BlockSpec((None, ha, c), lambda s, t: (s, 0, 0)),
                  pl.BlockSpec((None, hb, c), lambda s, t: (s, 0, 0)),
                  whole(conv_a), whole(conv_b), whole(bias), whole(ln_g), whole(ln_b),
                  pl.BlockSpec(memory_space=pl.ANY)],
        out_specs=[pl.BlockSpec((tc, 2 * c), lambda s, t: (blk0 + s * nt + t, 0)),
                   pl.BlockSpec((None, wa_len - 1, c), lambda s, t: (s, 0, 0)),
                   pl.BlockSpec((None, wb_len - 1, c), lambda s, t: (s, 0, 0))],
        out_shape=[jax.ShapeDtypeStruct(ybuf.shape, ybuf.dtype),
                   jax.ShapeDtypeStruct((n_seq, wa_len - 1, c), F32),
                   jax.ShapeDtypeStruct((n_seq, wb_len - 1, c), F32)],
        scratch_shapes=[pltpu.VMEM((ha + tc, c), F32), pltpu.VMEM((hb + tc, c), F32),
                        pltpu.VMEM((max(ha, hb) + tc, c), F32)],
        compiler_params=_params("arbitrary", "arbitrary"),
        input_output_aliases={12: 0},
        name="even_mixer",
    )(z, z, z, z, z, hist_a, hist_b, conv_a, conv_b, bias, ln_g, ln_b, ybuf)


def _lb_body(logit_ref, lb_ref):
    n = logit_ref.shape[0]
    rows = [logit_ref[j:j + 1, :] for j in range(n)]
    mx = functools.reduce(jnp.maximum, rows)
    es = [jnp.exp(r - mx) for r in rows]
    tot = functools.reduce(lambda a, b: a + b, es)
    first = es[0] / tot
    run = jnp.zeros_like(first)
    for j in range(n):
        run = run + es[j] / tot
        lb_ref[j:j + 1, :] = run - first


def _lower_bounds(logits):
    return pl.pallas_call(
        _lb_body,
        out_shape=jax.ShapeDtypeStruct(logits.shape, F32),
        name="hgrn_lower_bounds",
    )(logits)


def _odd_body(qc_ref, kc_ref, vc_ref, oc_ref, qd_ref, fd_ref, id_ref, gd_ref, gcol_ref, grow_ref,
              bcol_ref, brow_ref, nc_ref, nd_ref, lb_ref, c0_ref, n0_ref, m0_ref, s0_ref, ybuf_ref,
              y_ref, c1_ref, n1_ref, m1_ref, s1_ref, c_scr, n_scr, m_scr, s_scr, *, heads, blk):
    del ybuf_ref
    L = blk
    hd_all = heads * HEAD_DIM
    step = pl.program_id(1)

    @pl.when(step == 0)
    def _():
        c_scr[...] = c0_ref[...]
        n_scr[...] = n0_ref[...]
        m_scr[...] = m0_ref[...]
        s_scr[...] = s0_ref[...]

    row = lax.broadcasted_iota(jnp.int32, (L, L), 0)
    col = lax.broadcasted_iota(jnp.int32, (L, L), 1)
    causal = row >= col
    tri_l = causal.astype(BF16)
    tri_u = (row <= col).astype(BF16)

    def cumsum_rows(x):
        hi, mid, lo = _split3(x)
        return _dot(tri_l, hi) + _dot(tri_l, mid) + _dot(tri_l, lo)

    def cumsum_lanes(x):
        hi, mid, lo = _split3(x)
        return _dot(hi, tri_u) + _dot(mid, tri_u) + _dot(lo, tri_u)

    gcol = gcol_ref[...] + bcol_ref[...]
    grow = grow_ref[...] + brow_ref[...]
    bt_col = cumsum_rows(_log_sigmoid(gcol))
    bt_row = cumsum_lanes(_log_sigmoid(grow))
    scale = HEAD_DIM ** -0.5
    m_all = m_scr[...]
    n_all = n_scr[...]
    hr = range(heads)
    hsl = [slice(h * HEAD_DIM, (h + 1) * HEAD_DIM) for h in hr]
    b_c = [bt_col[:, heads + h:heads + h + 1] for h in hr]
    i_c = [gcol[:, h:h + 1] for h in hr]
    b_r = [bt_row[heads + h:heads + h + 1, :] for h in hr]
    i_r = [grow[h:h + 1, :] for h in hr]
    m_prev = [m_all[h:h + 1, 0:1] for h in hr]
    dmat = [jnp.where(causal, b_c[h] - b_r[h] + i_r[h], -jnp.inf) for h in hr]
    inter = [b_c[h] + m_prev[h] for h in hr]
    m_row = [jnp.maximum(inter[h], jnp.max(dmat[h], axis=-1, keepdims=True)) for h in hr]
    q = [qc_ref[:, hsl[h]] for h in hr]
    k = [kc_ref[:, hsl[h]] * scale for h in hr]
    qb = [q[h].astype(BF16) for h in hr]
    vb = [vc_ref[:, hsl[h]].astype(BF16) for h in hr]
    s_mat = [_dot_nt(qb[h], k[h].astype(BF16)) * jnp.exp(dmat[h] - m_row[h]) for h in hr]
    w_inter = [jnp.exp(inter[h] - m_row[h]) for h in hr]
    n_old = [n_all[h:h + 1, :] for h in hr]
    den = [jnp.sum(s_mat[h], axis=-1, keepdims=True)
           + w_inter[h] * jnp.sum(q[h] * n_old[h], axis=-1, keepdims=True) for h in hr]
    c_old = [c_scr[h] for h in hr]
    num = [_dot(s_mat[h].astype(BF16), vb[h]) + w_inter[h] * _dot(qb[h], c_old[h].astype(BF16)) for h in hr]
    hh = [num[h] / jnp.maximum(jnp.abs(den[h]), jnp.exp(-m_row[h])) for h in hr]
    ms = [jnp.mean(hh[h] * hh[h], axis=-1, keepdims=True) for h in hr]
    for h in hr:
        hn = hh[h] * lax.rsqrt(ms[h] + EPS) * nc_ref[:, hsl[h]]
        y_ref[:, hsl[h]] = (jax.nn.sigmoid(oc_ref[:, hsl[h]]) * hn).astype(y_ref.dtype)
    b_last = [b_r[h][:, L - 1:L] for h in hr]
    m_new = [jnp.maximum(b_last[h] + m_prev[h], jnp.max(b_last[h] - b_r[h] + i_r[h], axis=-1, keepdims=True))
             for h in hr]
    kw = [k[h] * jnp.exp(b_last[h] - b_c[h] + i_c[h] - m_new[h]) for h in hr]
    decay = [jnp.exp(b_last[h] + m_prev[h] - m_new[h]) for h in hr]
    for h in hr:
        c_scr[h] = decay[h] * c_old[h] + _dot_tn(kw[h].astype(BF16), vb[h])
    n_scr[...] = jnp.concatenate([decay[h] * n_old[h] + jnp.sum(kw[h], axis=0, keepdims=True) for h in hr], axis=0)
    m_scr[...] = jnp.concatenate([jnp.broadcast_to(m_new[h], (1, m_scr.shape[1])) for h in hr], axis=0)

    lb = lb_ref[...]
    f = lb + (1.0 - lb) * jax.nn.sigmoid(fd_ref[...])
    b_all = cumsum_rows(jnp.log(f))
    kk_all = 1.0 - f
    qd = qd_ref[...]
    qq_all = qd * jax.nn.sigmoid(qd)
    b_end = b_all[L - 1:L, :]
    q_in = (qq_all * jnp.exp(b_all)).astype(BF16)
    k_out = (kk_all * jnp.exp(b_end - b_all)).astype(BF16)
    s_decay = jnp.exp(b_end)
    vd = [id_ref[:, hsl[h]].astype(BF16) for h in hr]
    st_old = [s_scr[h] for h in hr]
    o_inter = [_dot_nt(q_in[:, hsl[h]], st_old[h].astype(BF16)) for h in hr]
    for h in hr:
        s_scr[h] = st_old[h] * s_decay[:, hsl[h]] + _dot_tn(vd[h], k_out[:, hsl[h]])

    row_hd = lax.broadcasted_iota(jnp.int32, (L, hd_all), 0)
    n8 = L // V7X_SUBLANES
    sub3 = lax.broadcasted_iota(jnp.int32, (n8, V7X_SUBLANES, hd_all), 1)
    b3 = b_all.reshape(n8, V7X_SUBLANES, hd_all)
    a_acc = [None for _ in hr]
    m = L // 2
    while m >= 1:
        if m >= V7X_SUBLANES:
            nb = L // (2 * m)
            pieces = [jnp.broadcast_to(b_all[i * 2 * m + m - 1:i * 2 * m + m, :], (2 * m, hd_all)) for i in range(nb)]
            anchor = jnp.concatenate(pieces, axis=0) if nb > 1 else pieces[0]
        else:
            anchor = None
            for i in range(V7X_SUBLANES // (2 * m)):
                piece = jnp.broadcast_to(b3[:, i * 2 * m + m - 1:i * 2 * m + m, :], b3.shape)
                anchor = piece if anchor is None else jnp.where(sub3 >= i * 2 * m, piece, anchor)
            anchor = anchor.reshape(L, hd_all)
        upper = (row_hd & m) != 0
        x_m = (jnp.where(upper, qq_all, kk_all) * jnp.exp(-jnp.abs(b_all - anchor))).astype(BF16)
        valid = ((row & m) != 0) & ((col & m) == 0) & ((row // (2 * m)) == (col // (2 * m)))
        for h in hr:
            a_m = jnp.where(valid, _dot_nt(x_m[:, hsl[h]], x_m[:, hsl[h]]), 0.0)
            a_acc[h] = a_m if a_acc[h] is None else a_acc[h] + a_m
        m //= 2
    q_0, k_0 = qq_all.astype(BF16), kk_all.astype(BF16)
    for h in hr:
        a_acc[h] = a_acc[h] + jnp.where(row == col, _dot_nt(q_0[:, hsl[h]], k_0[:, hsl[h]]), 0.0)

    o_all = [_dot(a_acc[h].astype(BF16), vd[h]) + o_inter[h] for h in hr]
    ms_d = [jnp.mean(o_all[h] * o_all[h], axis=-1, keepdims=True) for h in hr]
    for h in hr:
        on = o_all[h] * lax.rsqrt(ms_d[h] + EPS) * nd_ref[:, hsl[h]]
        y_ref[:, hd_all + h * HEAD_DIM: hd_all + (h + 1) * HEAD_DIM] = (
            jax.nn.sigmoid(gd_ref[:, hsl[h]]) * on).astype(y_ref.dtype)

    @pl.when(step == pl.num_programs(1) - 1)
    def _():
        c1_ref[...] = c_scr[...]
        n1_ref[...] = n_scr[...]
        m1_ref[...] = m_scr[...]
        s1_ref[...] = s_scr[...]


def _odd_mixer(z, gcol, grow3, row_off, n_seq, t_len, blk, bcol, brow, norm_c, norm_d, lb, c0, n0, m0, s0t, ybuf):
    heads = c0.shape[1]
    hd_all = heads * HEAD_DIM
    nblk = t_len // blk
    assert row_off % blk == 0 and t_len % blk == 0
    blk0 = row_off // blk
    gr = grow3.shape[1]

    def zspec(colblk):
        return pl.BlockSpec((blk, hd_all), lambda s, c: (blk0 + s * nblk + c, colblk))

    def whole(a):
        return pl.BlockSpec(a.shape, lambda s, c: (0,) * a.ndim)

    def per_seq(a):
        return pl.BlockSpec((None,) + a.shape[1:], lambda s, c: (s,) + (0,) * (a.ndim - 1))

    state_shapes = [jax.ShapeDtypeStruct(a.shape, F32) for a in (c0, n0, m0, s0t)]
    return pl.pallas_call(
        functools.partial(_odd_body, heads=heads, blk=blk),
        grid=(n_seq, nblk),
        in_specs=[zspec(i) for i in range(8)] + [
            pl.BlockSpec((blk, V7X_LANES), lambda s, c: (blk0 + s * nblk + c, 0)),
            pl.BlockSpec((None, gr, blk), lambda s, c: (s * nblk + c, 0, 0)),
            whole(bcol), whole(brow), whole(norm_c), whole(norm_d), whole(lb),
            per_seq(c0), per_seq(n0), per_seq(m0), per_seq(s0t), pl.BlockSpec(memory_space=pl.ANY)],
        out_specs=[pl.BlockSpec((blk, 2 * hd_all), lambda s, c: (blk0 + s * nblk + c, 0)),
                   per_seq(c0), per_seq(n0), per_seq(m0), per_seq(s0t)],
        out_shape=[jax.ShapeDtypeStruct(ybuf.shape, ybuf.dtype)] + state_shapes,
        scratch_shapes=[pltpu.VMEM(c0.shape[1:], F32), pltpu.VMEM(n0.shape[1:], F32),
                        pltpu.VMEM(m0.shape[1:], F32), pltpu.VMEM(s0t.shape[1:], F32)],
        compiler_params=_params("arbitrary", "arbitrary"),
        input_output_aliases={19: 0},
        name="odd_mixer",
    )(z, z, z, z, z, z, z, z, gcol, grow3, bcol, brow, norm_c, norm_d, lb, c0, n0, m0, s0t, ybuf)


def kernel(x_prompt, x_sample, state_conv_a, state_conv_b, state_mlstm_c, state_mlstm_n, state_mlstm_m,
           state_hgrn_s, norm_g, norm_f, ffn_w_gate, ffn_w_up, ffn_w_down, even_w_in, even_conv_a,
           even_conv_b, even_conv_b_bias, even_ln_g, even_ln_b, even_w_out, odd_w_in, odd_bias_i,
           odd_bias_f, odd_norm_c, odd_lb_logits, odd_norm_d, odd_w_out):
    bp, tp, d = x_prompt.shape
    bs, ts, _ = x_sample.shape
    mp, ms = bp * tp, bs * ts
    depth = norm_g.shape[0]
    c_a, c_b = even_conv_a.shape[-1], even_conv_b.shape[-1]
    wa_len, wb_len = even_conv_a.shape[1], even_conv_b.shape[1]
    heads = state_mlstm_c.shape[2]
    hd_all = heads * HEAD_DIM
    assert c_a == c_b and even_w_in.shape[-1] == 5 * c_a
    assert state_mlstm_c.shape[3:] == (HEAD_DIM, HEAD_DIM) and state_hgrn_s.shape[2:] == (heads, HEAD_DIM, HEAD_DIM)
    assert odd_w_in.shape[-1] == 8 * hd_all + 2 * heads and 2 * heads <= V7X_LANES

    x = jnp.concatenate([x_prompt.reshape(mp, d), x_sample.reshape(ms, d)], axis=0)

    w_ffn = (ffn_w_gate[0, 0].astype(BF16), ffn_w_up[0, 0].astype(BF16), ffn_w_down[0, 0].astype(BF16))
    ffn_stacks = (ffn_w_gate.astype(F32), ffn_w_up.astype(F32), ffn_w_down.astype(F32))
    ew_in, ew_out = even_w_in.astype(BF16), even_w_out.astype(BF16)
    ow_in, ow_out = odd_w_in.astype(BF16), odd_w_out.astype(BF16)
    w_gate_cols = odd_w_in[:, :, 8 * hd_all:]
    gr = -(-2 * heads // V7X_SUBLANES) * V7X_SUBLANES
    wgc = jnp.pad(w_gate_cols, ((0, 0), (0, 0), (0, V7X_LANES - 2 * heads))).astype(BF16)
    wgr = jnp.pad(jnp.swapaxes(w_gate_cols, 1, 2), ((0, 0), (0, gr - 2 * heads), (0, 0))).astype(BF16)
    gate_bias = jnp.concatenate([odd_bias_i, odd_bias_f], axis=-1).astype(F32)
    bcol = jnp.pad(gate_bias, ((0, 0), (0, V7X_LANES - 2 * heads)))[:, None, :]
    brow = jnp.pad(gate_bias, ((0, 0), (0, gr - 2 * heads)))[:, :, None]
    lb_all = _lower_bounds(odd_lb_logits.astype(F32))

    ha = -(-(wa_len - 1) // V7X_SUBLANES) * V7X_SUBLANES
    hb = -(-(wb_len - 1) // V7X_SUBLANES) * V7X_SUBLANES

    def hist(state, n, h_rows, w_len):
        if state is None:
            return jnp.zeros((n, h_rows, c_a), F32)
        return jnp.pad(state.astype(F32), ((0, 0), (h_rows - (w_len - 1), 0), (0, 0)))

    blk_p = _pow2_block(tp, CHUNK_CAP)
    blk_s = _pow2_block(ts, CHUNK_CAP)
    groups = ((0, bp, tp, blk_p), (mp, bs, ts, blk_s))

    assert c_a == hd_all
    y = jnp.zeros((mp + ms, 2 * c_a), BF16)
    new_a, new_b, new_c, new_n, new_m, new_s = ([[], []] for _ in range(6))
    for l in range(depth):
        j = l // 2
        x, w_ffn = _ffn(x, norm_g[l, 0][None, :], *w_ffn, nxt=ffn_stacks + (l, 1))
        if l % 2 == 0:
            z = _proj(x, norm_g[l, 1][None, :], ew_in, j, 5 * c_a)
            for gi, (off, n, t_len, _) in enumerate(groups):
                sa = None if gi == 0 else state_conv_a[j]
                sb = None if gi == 0 else state_conv_b[j]
                y, a1, b1 = _even_mixer(
                    z, off, n, t_len, hist(sa, n, ha, wa_len), hist(sb, n, hb, wb_len),
                    even_conv_a[j].astype(F32), even_conv_b[j].astype(F32),
                    even_conv_b_bias[j][None, :], even_ln_g[j][None, :], even_ln_b[j][None, :], y)
                new_a[gi].append(a1)
                new_b[gi].append(b1)
            x = _outproj(x, y, ew_out, j)
        else:
            z, gcol, grow = _proj(x, norm_g[l, 1][None, :], ow_in, j, 8 * hd_all, wgc, wgr)
            for gi, (off, n, t_len, blk) in enumerate(groups):
                grow3 = grow[:, off:off + n * t_len].reshape(gr, n * t_len // blk, blk).transpose(1, 0, 2)
                if gi == 0:
                    c0 = jnp.zeros((n, heads, HEAD_DIM, HEAD_DIM), F32)
                    n0 = jnp.zeros((n, heads, HEAD_DIM), F32)
                    m0 = jnp.zeros((n, heads, V7X_LANES), F32)
                    s0t = jnp.zeros((n, heads, HEAD_DIM, HEAD_DIM), F32)
                else:
                    c0 = state_mlstm_c[j].astype(F32)
                    n0 = state_mlstm_n[j].astype(F32)
                    m0 = jnp.broadcast_to(state_mlstm_m[j].astype(F32)[:, :, None], (n, heads, V7X_LANES))
                    s0t = jnp.swapaxes(state_hgrn_s[j].astype(F32), -1, -2)
                y, c1, n1, m1, s1t = _odd_mixer(
                    z, gcol, grow3, off, n, t_len, blk, bcol[j], brow[j], odd_norm_c[j][None, :],
                    odd_norm_d[j][None, :], lb_all[j][None, :], c0, n0, m0, s0t, y)
                new_c[gi].append(c1)
                new_n[gi].append(n1)
                new_m[gi].append(m1[:, :, 0])
                new_s[gi].append(jnp.swapaxes(s1t, -1, -2))
            x = _outproj(x, y, ow_out, j)
        if l == depth - 1:
            x, _ = _ffn(x, norm_g[l, 2][None, :], *w_ffn, gf=norm_f[None, :])
        else:
            x, w_ffn = _ffn(x, norm_g[l, 2][None, :], *w_ffn, nxt=ffn_stacks + (l + 1, 0))

    y_prompt = x[:mp].reshape(bp, tp, d)
    y_sample = x[mp:].reshape(bs, ts, d)
    st = lambda lst, gi: jnp.stack(lst[gi])
    return (y_prompt, y_sample,
            st(new_a, 0), st(new_b, 0), st(new_c, 0), st(new_n, 0), st(new_m, 0), st(new_s, 0),
            st(new_a, 1), st(new_b, 1), st(new_c, 1), st(new_n, 1), st(new_m, 1), st(new_s, 1))
```

```python
import functools

import jax
import jax.numpy as jnp
from jax import lax
from jax.experimental import pallas as pl
from jax.experimental.pallas import tpu as pltpu

F32 = jnp.float32
BF16 = jnp.bfloat16
EPS = 1e-6

V7X_LANES = 128
V7X_SUBLANES = 8
V7X_VMEM_LIMIT = 56 * 1024 * 1024
HEAD_DIM = 128
ROW_TILE_CAP = 768
FF_TILE_CAP = 512
PROJ_ROW_TILE_CAP = 1408
COL_TILE_CAP = 512
CONV_TILE_CAP = 256
CHUNK_CAP = 128


def _tile(n, cap, align):
    if n <= cap:
        return n
    for d in range(cap - cap % align, 0, -align):
        if n % d == 0:
            return d
    raise ValueError(f"no tile for {n} (cap {cap}, align {align})")


def _pow2_block(n, cap):
    b = V7X_SUBLANES
    assert n % b == 0
    while 2 * b <= cap and n % (2 * b) == 0:
        b *= 2
    return b


def _params(*sem):
    return pltpu.CompilerParams(dimension_semantics=sem, vmem_limit_bytes=V7X_VMEM_LIMIT)


def _rms(x, g):
    return x * lax.rsqrt(jnp.mean(x * x, axis=-1, keepdims=True) + EPS) * g


def _dot(a, b):
    return jnp.dot(a, b, preferred_element_type=F32)


def _dot_nt(a, b):
    return lax.dot_general(a, b, (((1,), (1,)), ((), ())), preferred_element_type=F32)


def _dot_tn(a, b):
    return lax.dot_general(a, b, (((0,), (0,)), ((), ())), preferred_element_type=F32)


def _split3(x):
    hi = x.astype(BF16)
    r1 = x - hi.astype(F32)
    mid = r1.astype(BF16)
    lo = (r1 - mid.astype(F32)).astype(BF16)
    return hi, mid, lo


def _log_sigmoid(x):
    return jnp.minimum(x, 0.0) - jnp.log1p(jnp.exp(-jnp.abs(x)))


def _ffn_body(x_ref, g_ref, wg_ref, wu_ref, wd_ref, *rest, final_norm, cast_next):
    rest = list(rest)
    gf_ref = rest.pop(0) if final_norm else None
    nxt_in = [rest.pop(0) for _ in range(3)] if cast_next else []
    o_ref = rest.pop(0)
    nxt_out = [rest.pop(0) for _ in range(3)] if cast_next else []
    (xn_ref,) = rest
    j = pl.program_id(1)

    @pl.when(j == 0)
    def _():
        xn_ref[...] = _rms(x_ref[...], g_ref[...]).astype(BF16)
        o_ref[...] = jnp.zeros_like(o_ref)

    xn = xn_ref[...]
    hg = _dot(xn, wg_ref[...])
    hu = _dot(xn, wu_ref[...])
    a = (hg * jax.nn.sigmoid(hg) * hu).astype(BF16)
    o_ref[...] += _dot(a, wd_ref[...])

    @pl.when(j == pl.num_programs(1) - 1)
    def _():
        y = x_ref[...] + 0.5 * o_ref[...]
        if final_norm:
            y = _rms(y, gf_ref[...])
        o_ref[...] = y

    for src, dst in zip(nxt_in, nxt_out):
        dst[...] = src[...].astype(BF16)


def _ffn(x, g, wg, wu, wd, gf=None, nxt=None):
    m, d = x.shape
    f = wg.shape[-1]
    tm = _tile(m, ROW_TILE_CAP, V7X_LANES)
    tf = _tile(f, FF_TILE_CAP, V7X_LANES)
    n_i = m // tm
    in_specs = [
        pl.BlockSpec((tm, d), lambda i, j: (i, 0)),
        pl.BlockSpec((1, d), lambda i, j: (0, 0)),
        pl.BlockSpec((d, tf), lambda i, j: (0, j)),
        pl.BlockSpec((d, tf), lambda i, j: (0, j)),
        pl.BlockSpec((tf, d), lambda i, j: (j, 0)),
    ]
    out_specs = [pl.BlockSpec((tm, d), lambda i, j: (i, 0))]
    out_shape = [jax.ShapeDtypeStruct((m, d), F32)]
    args = [x, g, wg, wu, wd]
    if gf is not None:
        in_specs.append(pl.BlockSpec((1, d), lambda i, j: (0, 0)))
        args.append(gf)
    if nxt is not None:
        ng, nu, nd, ln, kn = nxt
        rb = next(r for r in range(V7X_LANES, d + 1, V7X_LANES) if d % r == 0 and d // r <= n_i)
        last_r, last_c = d // rb - 1, f // tf - 1

        def blk_r(i, j):
            return jnp.minimum(i, last_r)

        def blk_c(i, j):
            return jnp.where(i > last_r, last_c, j)

        in_specs += [
            pl.BlockSpec((None, None, rb, tf), lambda i, j: (ln, kn, blk_r(i, j), blk_c(i, j))),
            pl.BlockSpec((None, None, rb, tf), lambda i, j: (ln, kn, blk_r(i, j), blk_c(i, j))),
            pl.BlockSpec((None, None, tf, rb), lambda i, j: (ln, kn, blk_c(i, j), blk_r(i, j))),
        ]
        out_specs += [
            pl.BlockSpec((rb, tf), lambda i, j: (blk_r(i, j), blk_c(i, j))),
            pl.BlockSpec((rb, tf), lambda i, j: (blk_r(i, j), blk_c(i, j))),
            pl.BlockSpec((tf, rb), lambda i, j: (blk_c(i, j), blk_r(i, j))),
        ]
        out_shape += [jax.ShapeDtypeStruct((d, f), BF16), jax.ShapeDtypeStruct((d, f), BF16),
                      jax.ShapeDtypeStruct((f, d), BF16)]
        args += [ng, nu, nd]
    outs = pl.pallas_call(
        functools.partial(_ffn_body, final_norm=gf is not None, cast_next=nxt is not None),
        grid=(n_i, f // tf),
        in_specs=in_specs,
        out_specs=out_specs,
        out_shape=out_shape,
        scratch_shapes=[pltpu.VMEM((tm, d), BF16)],
        compiler_params=_params("arbitrary", "arbitrary"),
        name="ffn_final" if gf is not None else "ffn",
    )(*args)
    return outs[0], tuple(outs[1:])


def _proj_body(x_ref, g_ref, w_ref, *rest, gates):
    if gates:
        wgc_ref, wgr_ref, z_ref, gc_ref, gr_ref, xn_ref = rest
    else:
        z_ref, xn_ref = rest

    @pl.when(pl.program_id(1) == 0)
    def _():
        xn = _rms(x_ref[...], g_ref[...]).astype(BF16)
        xn_ref[...] = xn
        if gates:
            gc_ref[...] = _dot(xn, wgc_ref[...])
            gr_ref[...] = _dot_nt(wgr_ref[...], xn)

    z_ref[...] = _dot(xn_ref[...], w_ref[...])


def _proj(x, g, w, jl, n, wgc=None, wgr=None):
    m, d = x.shape
    tm = _tile(m, PROJ_ROW_TILE_CAP, V7X_LANES)
    tn = _tile(n, COL_TILE_CAP, V7X_LANES)
    gates = wgc is not None
    in_specs = [
        pl.BlockSpec((tm, d), lambda i, j: (i, 0)),
        pl.BlockSpec((1, d), lambda i, j: (0, 0)),
        pl.BlockSpec((None, d, tn), lambda i, j: (jl, 0, j)),
    ]
    out_specs = [pl.BlockSpec((tm, tn), lambda i, j: (i, j))]
    out_shape = [jax.ShapeDtypeStruct((m, n), F32)]
    args = [x, g, w]
    if gates:
        gr = wgr.shape[1]
        in_specs += [
            pl.BlockSpec((None, d, V7X_LANES), lambda i, j: (jl, 0, 0)),
            pl.BlockSpec((None, gr, d), lambda i, j: (jl, 0, 0)),
        ]
        out_specs += [
            pl.BlockSpec((tm, V7X_LANES), lambda i, j: (i, 0)),
            pl.BlockSpec((gr, tm), lambda i, j: (0, i)),
        ]
        out_shape += [jax.ShapeDtypeStruct((m, V7X_LANES), F32), jax.ShapeDtypeStruct((gr, m), F32)]
        args += [wgc, wgr]
    outs = pl.pallas_call(
        functools.partial(_proj_body, gates=gates),
        grid=(m // tm, n // tn),
        in_specs=in_specs,
        out_specs=out_specs,
        out_shape=out_shape,
        scratch_shapes=[pltpu.VMEM((tm, d), BF16)],
        compiler_params=_params("parallel", "arbitrary"),
        name="proj_gates" if gates else "proj",
    )(*args)
    return outs if gates else outs[0]


def _outproj_body(x_ref, y_ref, w_ref, o_ref):
    o_ref[...] = x_ref[...] + _dot(y_ref[...], w_ref[...])


def _outproj(x, y, w, jl):
    m, d = x.shape
    kdim = y.shape[1]
    tm = _tile(m, ROW_TILE_CAP, V7X_LANES)
    return pl.pallas_call(
        _outproj_body,
        grid=(m // tm,),
        in_specs=[
            pl.BlockSpec((tm, d), lambda i: (i, 0)),
            pl.BlockSpec((tm, kdim), lambda i: (i, 0)),
            pl.BlockSpec((None, kdim, d), lambda i: (jl, 0, 0), pipeline_mode=pl.Buffered(1)),
        ],
        out_specs=pl.BlockSpec((tm, d), lambda i: (i, 0)),
        out_shape=jax.ShapeDtypeStruct((m, d), F32),
        compiler_params=_params("parallel"),
        name="outproj",
    )(x, y, w)


def _even_body(z_ref, ha_ref, hb_ref, wa_ref, wb_ref, bias_ref, lng_ref,
               lnb_ref, ybuf_ref, y_ref, na_ref, nb_ref, ua_scr, ub_scr, sh_scr, *, tc, wa_len, wb_len, ha, hb):
    del ybuf_ref
    t = pl.program_id(1)
    c = wa_ref.shape[1]
    xa_ref, gb_ref, gc_ref, gv_ref, gg_ref = (z_ref.at[:, k * c:(k + 1) * c] for k in range(5))

    @pl.when(t == 0)
    def _():
        ua_scr[0:ha, :] = ha_ref[...]
        ub_scr[0:hb, :] = hb_ref[...]

    @pl.when(t > 0)
    def _():
        ua_scr[0:ha, :] = ua_scr[tc:tc + ha, :]
        ub_scr[0:hb, :] = ub_scr[tc:tc + hb, :]

    ua_scr[ha:ha + tc, :] = gc_ref[...] * xa_ref[...]
    ub_scr[hb:hb + tc, :] = gv_ref[...] * jax.nn.sigmoid(gg_ref[...])

    def conv(w_ref, u_scr, first):
        n_taps = w_ref.shape[0]
        acc = jnp.zeros((tc, c), F32)
        for r in range(V7X_SUBLANES):
            offs = [o for o in range(first, first + n_taps) if o % V7X_SUBLANES == r]
            if not offs:
                continue
            span = max(offs) - r
            if r == 0:
                src = u_scr
            else:
                sh_scr[0:span + tc, :] = u_scr[r:r + span + tc, :]
                src = sh_scr
            for o in offs:
                acc = acc + w_ref[o - first:o - first + 1, :] * src[o - r:o - r + tc, :]
        return acc

    ca = conv(wa_ref, ua_scr, ha - (wa_len - 1))
    y_ref[:, 0:c] = (gb_ref[...] * ca).astype(y_ref.dtype)

    cb = conv(wb_ref, ub_scr, hb - (wb_len - 1))
    cb = cb + bias_ref[...]
    mu = jnp.mean(cb, axis=-1, keepdims=True)
    dc = cb - mu
    var = jnp.mean(dc * dc, axis=-1, keepdims=True)
    ln = dc * lax.rsqrt(var + EPS) * lng_ref[...] + lnb_ref[...]
    y_ref[:, c:2 * c] = (ln * jax.nn.sigmoid(ln)).astype(y_ref.dtype)

    @pl.when(t == pl.num_programs(1) - 1)
    def _():
        na_ref[...] = ua_scr[ha + tc - (wa_len - 1): ha + tc, :]
        nb_ref[...] = ub_scr[hb + tc - (wb_len - 1): hb + tc, :]


def _even_mixer(z, row_off, n_seq, t_len, hist_a, hist_b, conv_a, conv_b, bias, ln_g, ln_b, ybuf):
    c = conv_a.shape[-1]
    wa_len, wb_len = conv_a.shape[0], conv_b.shape[0]
    ha, hb = hist_a.shape[1], hist_b.shape[1]
    tc = _tile(t_len, CONV_TILE_CAP, V7X_SUBLANES)
    nt = t_len // tc
    assert row_off % tc == 0 and tc >= hb and tc >= ha
    blk0 = row_off // tc

    def whole(a):
        return pl.BlockSpec(a.shape, lambda s, t: (0,) * a.ndim)

    return pl.pallas_call(
        functools.partial(_even_body, tc=tc, wa_len=wa_len, wb_len=wb_len, ha=ha, hb=hb),
        grid=(n_seq, nt),
        in_specs=[pl.BlockSpec((tc, 5 * c), lambda s, t: (blk0 + s * nt + t, 0)),
                  pl.BlockSpec((None, ha, c), lambda s, t: (s, 0, 0)),
                  pl.BlockSpec((None, hb, c), lambda s, t: (s, 0, 0)),
                  whole(conv_a), whole(conv_b), whole(bias), whole(ln_g), whole(ln_b),
                  pl.BlockSpec(memory_space=pl.ANY)],
        out_specs=[pl.BlockSpec((tc, 2 * c), lambda s, t: (blk0 + s * nt + t, 0)),
                   pl.BlockSpec((None, wa_len - 1, c), lambda s, t: (s, 0, 0)),
                   pl.BlockSpec((None, wb_len - 1, c), lambda s, t: (s, 0, 0))],
        out_shape=[jax.ShapeDtypeStruct(ybuf.shape, ybuf.dtype),
                   jax.ShapeDtypeStruct((n_seq, wa_len - 1, c), F32),
                   jax.ShapeDtypeStruct((n_seq, wb_len - 1, c), F32)],
        scratch_shapes=[pltpu.VMEM((ha + tc, c), F32), pltpu.VMEM((hb + tc, c), F32),
                        pltpu.VMEM((max(ha, hb) + tc, c), F32)],
        compiler_params=_params("arbitrary", "arbitrary"),
        input_output_aliases={8: 0},
        name="even_mixer",
    )(z, hist_a, hist_b, conv_a, conv_b, bias, ln_g, ln_b, ybuf)


def _lb_body(logit_ref, lb_ref):
    n = logit_ref.shape[0]
    rows = [logit_ref[j:j + 1, :] for j in range(n)]
    mx = functools.reduce(jnp.maximum, rows)
    es = [jnp.exp(r - mx) for r in rows]
    tot = functools.reduce(lambda a, b: a + b, es)
    first = es[0] / tot
    run = jnp.zeros_like(first)
    for j in range(n):
        run = run + es[j] / tot
        lb_ref[j:j + 1, :] = run - first


def _lower_bounds(logits):
    return pl.pallas_call(
        _lb_body,
        out_shape=jax.ShapeDtypeStruct(logits.shape, F32),
        name="hgrn_lower_bounds",
    )(logits)


def _odd_body(z_ref, gcol_ref, grow_ref,
              bcol_ref, brow_ref, nc_ref, nd_ref, lb_ref, c0_ref, n0_ref, m0_ref, s0_ref, ybuf_ref,
              y_ref, c1_ref, n1_ref, m1_ref, s1_ref, c_scr, n_scr, m_scr, s_scr, *, heads, blk):
    del ybuf_ref
    L = blk
    hd_all = heads * HEAD_DIM
    step = pl.program_id(1)
    qc_ref, kc_ref, vc_ref, oc_ref, qd_ref, fd_ref, id_ref, gd_ref = (
        z_ref.at[:, k * hd_all:(k + 1) * hd_all] for k in range(8))

    @pl.when(step == 0)
    def _():
        c_scr[...] = c0_ref[...]
        n_scr[...] = n0_ref[...]
        m_scr[...] = m0_ref[...]
        s_scr[...] = s0_ref[...]

    row = lax.broadcasted_iota(jnp.int32, (L, L), 0)
    col = lax.broadcasted_iota(jnp.int32, (L, L), 1)
    causal = row >= col
    tri_l = causal.astype(BF16)
    tri_u = (row <= col).astype(BF16)

    def cumsum_rows(x):
        hi, mid, lo = _split3(x)
        return _dot(tri_l, hi) + _dot(tri_l, mid) + _dot(tri_l, lo)

    def cumsum_lanes(x):
        hi, mid, lo = _split3(x)
        return _dot(hi, tri_u) + _dot(mid, tri_u) + _dot(lo, tri_u)

    gcol = gcol_ref[...] + bcol_ref[...]
    grow = grow_ref[...] + brow_ref[...]
    bt_col = cumsum_rows(_log_sigmoid(gcol))
    bt_row = cumsum_lanes(_log_sigmoid(grow))
    scale = HEAD_DIM ** -0.5
    m_all = m_scr[...]
    n_all = n_scr[...]
    hr = range(heads)
    hsl = [slice(h * HEAD_DIM, (h + 1) * HEAD_DIM) for h in hr]
    b_c = [bt_col[:, heads + h:heads + h + 1] for h in hr]
    i_c = [gcol[:, h:h + 1] for h in hr]
    b_r = [bt_row[heads + h:heads + h + 1, :] for h in hr]
    i_r = [grow[h:h + 1, :] for h in hr]
    m_prev = [m_all[h:h + 1, 0:1] for h in hr]
    dmat = [jnp.where(causal, b_c[h] - b_r[h] + i_r[h], -jnp.inf) for h in hr]
    inter = [b_c[h] + m_prev[h] for h in hr]
    m_row = [jnp.maximum(inter[h], jnp.max(dmat[h], axis=-1, keepdims=True)) for h in hr]
    q = [qc_ref[:, hsl[h]] for h in hr]
    k = [kc_ref[:, hsl[h]] * scale for h in hr]
    qb = [q[h].astype(BF16) for h in hr]
    vb = [vc_ref[:, hsl[h]].astype(BF16) for h in hr]
    s_mat = [_dot_nt(qb[h], k[h].astype(BF16)) * jnp.exp(dmat[h] - m_row[h]) for h in hr]
    w_inter = [jnp.exp(inter[h] - m_row[h]) for h in hr]
    n_old = [n_all[h:h + 1, :] for h in hr]
    den = [jnp.sum(s_mat[h], axis=-1, keepdims=True)
           + w_inter[h] * jnp.sum(q[h] * n_old[h], axis=-1, keepdims=True) for h in hr]
    c_old = [c_scr[h] for h in hr]
    num = [_dot(s_mat[h].astype(BF16), vb[h]) + w_inter[h] * _dot(qb[h], c_old[h].astype(BF16)) for h in hr]
    hh = [num[h] / jnp.maximum(jnp.abs(den[h]), jnp.exp(-m_row[h])) for h in hr]
    ms = [jnp.mean(hh[h] * hh[h], axis=-1, keepdims=True) for h in hr]
    for h in hr:
        hn = hh[h] * lax.rsqrt(ms[h] + EPS) * nc_ref[:, hsl[h]]
        y_ref[:, hsl[h]] = (jax.nn.sigmoid(oc_ref[:, hsl[h]]) * hn).astype(y_ref.dtype)
    b_last = [b_r[h][:, L - 1:L] for h in hr]
    m_new = [jnp.maximum(b_last[h] + m_prev[h], jnp.max(b_last[h] - b_r[h] + i_r[h], axis=-1, keepdims=True))
             for h in hr]
    kw = [k[h] * jnp.exp(b_last[h] - b_c[h] + i_c[h] - m_new[h]) for h in hr]
    decay = [jnp.exp(b_last[h] + m_prev[h] - m_new[h]) for h in hr]
    for h in hr:
        c_scr[h] = decay[h] * c_old[h] + _dot_tn(kw[h].astype(BF16), vb[h])
    n_scr[...] = jnp.concatenate([decay[h] * n_old[h] + jnp.sum(kw[h], axis=0, keepdims=True) for h in hr], axis=0)
    m_scr[...] = jnp.concatenate([jnp.broadcast_to(m_new[h], (1, m_scr.shape[1])) for h in hr], axis=0)

    lb = lb_ref[...]
    f = lb + (1.0 - lb) * jax.nn.sigmoid(fd_ref[...])
    b_all = cumsum_rows(jnp.log(f))
    kk_all = 1.0 - f
    qd = qd_ref[...]
    qq_all = qd * jax.nn.sigmoid(qd)
    b_end = b_all[L - 1:L, :]
    q_in = (qq_all * jnp.exp(b_all)).astype(BF16)
    k_out = (kk_all * jnp.exp(b_end - b_all)).astype(BF16)
    s_decay = jnp.exp(b_end)
    vd = [id_ref[:, hsl[h]].astype(BF16) for h in hr]
    st_old = [s_scr[h] for h in hr]
    o_inter = [_dot_nt(q_in[:, hsl[h]], st_old[h].astype(BF16)) for h in hr]
    for h in hr:
        s_scr[h] = st_old[h] * s_decay[:, hsl[h]] + _dot_tn(vd[h], k_out[:, hsl[h]])

    row_hd = lax.broadcasted_iota(jnp.int32, (L, hd_all), 0)
    n8 = L // V7X_SUBLANES
    sub3 = lax.broadcasted_iota(jnp.int32, (n8, V7X_SUBLANES, hd_all), 1)
    b3 = b_all.reshape(n8, V7X_SUBLANES, hd_all)
    a_acc = [None for _ in hr]
    m = L // 2
    while m >= 1:
        if m >= V7X_SUBLANES:
            nb = L // (2 * m)
            pieces = [jnp.broadcast_to(b_all[i * 2 * m + m - 1:i * 2 * m + m, :], (2 * m, hd_all)) for i in range(nb)]
            anchor = jnp.concatenate(pieces, axis=0) if nb > 1 else pieces[0]
        else:
            anchor = None
            for i in range(V7X_SUBLANES // (2 * m)):
                piece = jnp.broadcast_to(b3[:, i * 2 * m + m - 1:i * 2 * m + m, :], b3.shape)
                anchor = piece if anchor is None else jnp.where(sub3 >= i * 2 * m, piece, anchor)
            anchor = anchor.reshape(L, hd_all)
        upper = (row_hd & m) != 0
        x_m = (jnp.where(upper, qq_all, kk_all) * jnp.exp(-jnp.abs(b_all - anchor))).astype(BF16)
        valid = ((row & m) != 0) & ((col & m) == 0) & ((row // (2 * m)) == (col // (2 * m)))
        for h in hr:
            a_m = jnp.where(valid, _dot_nt(x_m[:, hsl[h]], x_m[:, hsl[h]]), 0.0)
            a_acc[h] = a_m if a_acc[h] is None else a_acc[h] + a_m
        m //= 2
    q_0, k_0 = qq_all.astype(BF16), kk_all.astype(BF16)
    for h in hr:
        a_acc[h] = a_acc[h] + jnp.where(row == col, _dot_nt(q_0[:, hsl[h]], k_0[:, hsl[h]]), 0.0)

    o_all = [_dot(a_acc[h].astype(BF16), vd[h]) + o_inter[h] for h in hr]
    ms_d = [jnp.mean(o_all[h] * o_all[h], axis=-1, keepdims=True) for h in hr]
    for h in hr:
        on = o_all[h] * lax.rsqrt(ms_d[h] + EPS) * nd_ref[:, hsl[h]]
        y_ref[:, hd_all + h * HEAD_DIM: hd_all + (h + 1) * HEAD_DIM] = (
            jax.nn.sigmoid(gd_ref[:, hsl[h]]) * on).astype(y_ref.dtype)

    @pl.when(step == pl.num_programs(1) - 1)
    def _():
        c1_ref[...] = c_scr[...]
        n1_ref[...] = n_scr[...]
        m1_ref[...] = m_scr[...]
        s1_ref[...] = s_scr[...]


def _odd_mixer(z, gcol, grow3, row_off, n_seq, t_len, blk, bcol, brow, norm_c, norm_d, lb, c0, n0, m0, s0t, ybuf):
    heads = c0.shape[1]
    hd_all = heads * HEAD_DIM
    nblk = t_len // blk
    assert row_off % blk == 0 and t_len % blk == 0
    blk0 = row_off // blk
    gr = grow3.shape[1]

    def whole(a):
        return pl.BlockSpec(a.shape, lambda s, c: (0,) * a.ndim)

    def per_seq(a):
        return pl.BlockSpec((None,) + a.shape[1:], lambda s, c: (s,) + (0,) * (a.ndim - 1))

    state_shapes = [jax.ShapeDtypeStruct(a.shape, F32) for a in (c0, n0, m0, s0t)]
    return pl.pallas_call(
        functools.partial(_odd_body, heads=heads, blk=blk),
        grid=(n_seq, nblk),
        in_specs=[
            pl.BlockSpec((blk, 8 * hd_all), lambda s, c: (blk0 + s * nblk + c, 0)),
            pl.BlockSpec((blk, V7X_LANES), lambda s, c: (blk0 + s * nblk + c, 0)),
            pl.BlockSpec((None, gr, blk), lambda s, c: (s * nblk + c, 0, 0)),
            whole(bcol), whole(brow), whole(norm_c), whole(norm_d), whole(lb),
            per_seq(c0), per_seq(n0), per_seq(m0), per_seq(s0t), pl.BlockSpec(memory_space=pl.ANY)],
        out_specs=[pl.BlockSpec((blk, 2 * hd_all), lambda s, c: (blk0 + s * nblk + c, 0)),
                   per_seq(c0), per_seq(n0), per_seq(m0), per_seq(s0t)],
        out_shape=[jax.ShapeDtypeStruct(ybuf.shape, ybuf.dtype)] + state_shapes,
        scratch_shapes=[pltpu.VMEM(c0.shape[1:], F32), pltpu.VMEM(n0.shape[1:], F32),
                        pltpu.VMEM(m0.shape[1:], F32), pltpu.VMEM(s0t.shape[1:], F32)],
        compiler_params=_params("arbitrary", "arbitrary"),
        input_output_aliases={12: 0},
        name="odd_mixer",
    )(z, gcol, grow3, bcol, brow, norm_c, norm_d, lb, c0, n0, m0, s0t, ybuf)


def kernel(x_prompt, x_sample, state_conv_a, state_conv_b, state_mlstm_c, state_mlstm_n, state_mlstm_m,
           state_hgrn_s, norm_g, norm_f, ffn_w_gate, ffn_w_up, ffn_w_down, even_w_in, even_conv_a,
           even_conv_b, even_conv_b_bias, even_ln_g, even_ln_b, even_w_out, odd_w_in, odd_bias_i,
           odd_bias_f, odd_norm_c, odd_lb_logits, odd_norm_d, odd_w_out):
    bp, tp, d = x_prompt.shape
    bs, ts, _ = x_sample.shape
    mp, ms = bp * tp, bs * ts
    depth = norm_g.shape[0]
    c_a, c_b = even_conv_a.shape[-1], even_conv_b.shape[-1]
    wa_len, wb_len = even_conv_a.shape[1], even_conv_b.shape[1]
    heads = state_mlstm_c.shape[2]
    hd_all = heads * HEAD_DIM
    assert c_a == c_b and even_w_in.shape[-1] == 5 * c_a
    assert state_mlstm_c.shape[3:] == (HEAD_DIM, HEAD_DIM) and state_hgrn_s.shape[2:] == (heads, HEAD_DIM, HEAD_DIM)
    assert odd_w_in.shape[-1] == 8 * hd_all + 2 * heads and 2 * heads <= V7X_LANES

    x = jnp.concatenate([x_prompt.reshape(mp, d), x_sample.reshape(ms, d)], axis=0)

    w_ffn = (ffn_w_gate[0, 0].astype(BF16), ffn_w_up[0, 0].astype(BF16), ffn_w_down[0, 0].astype(BF16))
    ffn_stacks = (ffn_w_gate.astype(F32), ffn_w_up.astype(F32), ffn_w_down.astype(F32))
    ew_in, ew_out = even_w_in.astype(BF16), even_w_out.astype(BF16)
    ow_in, ow_out = odd_w_in.astype(BF16), odd_w_out.astype(BF16)
    w_gate_cols = odd_w_in[:, :, 8 * hd_all:]
    gr = -(-2 * heads // V7X_SUBLANES) * V7X_SUBLANES
    wgc = jnp.pad(w_gate_cols, ((0, 0), (0, 0), (0, V7X_LANES - 2 * heads))).astype(BF16)
    wgr = jnp.pad(jnp.swapaxes(w_gate_cols, 1, 2), ((0, 0), (0, gr - 2 * heads), (0, 0))).astype(BF16)
    gate_bias = jnp.concatenate([odd_bias_i, odd_bias_f], axis=-1).astype(F32)
    bcol = jnp.pad(gate_bias, ((0, 0), (0, V7X_LANES - 2 * heads)))[:, None, :]
    brow = jnp.pad(gate_bias, ((0, 0), (0, gr - 2 * heads)))[:, :, None]
    lb_all = _lower_bounds(odd_lb_logits.astype(F32))

    ha = -(-(wa_len - 1) // V7X_SUBLANES) * V7X_SUBLANES
    hb = -(-(wb_len - 1) // V7X_SUBLANES) * V7X_SUBLANES

    def hist(state, n, h_rows, w_len):
        if state is None:
            return jnp.zeros((n, h_rows, c_a), F32)
        return jnp.pad(state.astype(F32), ((0, 0), (h_rows - (w_len - 1), 0), (0, 0)))

    blk_p = _pow2_block(tp, CHUNK_CAP)
    blk_s = _pow2_block(ts, CHUNK_CAP)
    groups = ((0, bp, tp, blk_p), (mp, bs, ts, blk_s))

    assert c_a == hd_all
    y = jnp.zeros((mp + ms, 2 * c_a), BF16)
    new_a, new_b, new_c, new_n, new_m, new_s = ([[], []] for _ in range(6))
    for l in range(depth):
        j = l // 2
        x, w_ffn = _ffn(x, norm_g[l, 0][None, :], *w_ffn, nxt=ffn_stacks + (l, 1))
        if l % 2 == 0:
            z = _proj(x, norm_g[l, 1][None, :], ew_in, j, 5 * c_a)
            for gi, (off, n, t_len, _) in enumerate(groups):
                sa = None if gi == 0 else state_conv_a[j]
                sb = None if gi == 0 else state_conv_b[j]
                y, a1, b1 = _even_mixer(
                    z, off, n, t_len, hist(sa, n, ha, wa_len), hist(sb, n, hb, wb_len),
                    even_conv_a[j].astype(F32), even_conv_b[j].astype(F32),
                    even_conv_b_bias[j][None, :], even_ln_g[j][None, :], even_ln_b[j][None, :], y)
                new_a[gi].append(a1)
                new_b[gi].append(b1)
            x = _outproj(x, y, ew_out, j)
        else:
            z, gcol, grow = _proj(x, norm_g[l, 1][None, :], ow_in, j, 8 * hd_all, wgc, wgr)
            for gi, (off, n, t_len, blk) in enumerate(groups):
                grow3 = grow[:, off:off + n * t_len].reshape(gr, n * t_len // blk, blk).transpose(1, 0, 2)
                if gi == 0:
                    c0 = jnp.zeros((n, heads, HEAD_DIM, HEAD_DIM), F32)
                    n0 = jnp.zeros((n, heads, HEAD_DIM), F32)
                    m0 = jnp.zeros((n, heads, V7X_LANES), F32)
                    s0t = jnp.zeros((n, heads, HEAD_DIM, HEAD_DIM), F32)
                else:
                    c0 = state_mlstm_c[j].astype(F32)
                    n0 = state_mlstm_n[j].astype(F32)
                    m0 = jnp.broadcast_to(state_mlstm_m[j].astype(F32)[:, :, None], (n, heads, V7X_LANES))
                    s0t = jnp.swapaxes(state_hgrn_s[j].astype(F32), -1, -2)
                y, c1, n1, m1, s1t = _odd_mixer(
                    z, gcol, grow3, off, n, t_len, blk, bcol[j], brow[j], odd_norm_c[j][None, :],
                    odd_norm_d[j][None, :], lb_all[j][None, :], c0, n0, m0, s0t, y)
                new_c[gi].append(c1)
                new_n[gi].append(n1)
                new_m[gi].append(m1[:, :, 0])
                new_s[gi].append(jnp.swapaxes(s1t, -1, -2))
            x = _outproj(x, y, ow_out, j)
        if l == depth - 1:
            x, _ = _ffn(x, norm_g[l, 2][None, :], *w_ffn, gf=norm_f[None, :])
        else:
            x, w_ffn = _ffn(x, norm_g[l, 2][None, :], *w_ffn, nxt=ffn_stacks + (l + 1, 0))

    y_prompt = x[:mp].reshape(bp, tp, d)
    y_sample = x[mp:].reshape(bs, ts, d)
    st = lambda lst, gi: jnp.stack(lst[gi])
    return (y_prompt, y_sample,
            st(new_a, 0), st(new_b, 0), st(new_c, 0), st(new_n, 0), st(new_m, 0), st(new_s, 0),
            st(new_a, 1), st(new_b, 1), st(new_c, 1), st(new_n, 1), st(new_m, 1), st(new_s, 1))
```

```python
import functools

import jax
import jax.numpy as jnp
from jax import lax
from jax.experimental import pallas as pl
from jax.experimental.pallas import tpu as pltpu

F32 = jnp.float32
BF16 = jnp.bfloat16
EPS = 1e-6

V7X_LANES = 128
V7X_SUBLANES = 8
V7X_VMEM_LIMIT = 56 * 1024 * 1024
HEAD_DIM = 128
ROW_TILE_CAP = 768
FF_TILE_CAP = 512
PROJ_ROW_TILE = 128
COL_TILE_CAP = 1024
CONV_TILE_CAP = 256
CHUNK_CAP = 128


def _tile(n, cap, align):
    if n <= cap:
        return n
    for d in range(cap - cap % align, 0, -align):
        if n % d == 0:
            return d
    raise ValueError(f"no tile for {n} (cap {cap}, align {align})")


def _pow2_block(n, cap):
    b = V7X_SUBLANES
    assert n % b == 0
    while 2 * b <= cap and n % (2 * b) == 0:
        b *= 2
    return b


def _params(*sem):
    return pltpu.CompilerParams(dimension_semantics=sem, vmem_limit_bytes=V7X_VMEM_LIMIT)


def _rms(x, g):
    return x * lax.rsqrt(jnp.mean(x * x, axis=-1, keepdims=True) + EPS) * g


def _dot(a, b):
    return jnp.dot(a, b, preferred_element_type=F32)


def _dot_nt(a, b):
    return lax.dot_general(a, b, (((1,), (1,)), ((), ())), preferred_element_type=F32)


def _dot_tn(a, b):
    return lax.dot_general(a, b, (((0,), (0,)), ((), ())), preferred_element_type=F32)


def _split3(x):
    hi = x.astype(BF16)
    r1 = x - hi.astype(F32)
    mid = r1.astype(BF16)
    lo = (r1 - mid.astype(F32)).astype(BF16)
    return hi, mid, lo


def _log_sigmoid(x):
    return jnp.minimum(x, 0.0) - jnp.log1p(jnp.exp(-jnp.abs(x)))


def _ffn_body(x_ref, g_ref, wg_ref, wu_ref, wd_ref, *rest, final_norm, cast_next):
    rest = list(rest)
    gf_ref = rest.pop(0) if final_norm else None
    nxt_in = [rest.pop(0) for _ in range(3)] if cast_next else []
    o_ref = rest.pop(0)
    nxt_out = [rest.pop(0) for _ in range(3)] if cast_next else []
    (xn_ref,) = rest
    j = pl.program_id(1)

    @pl.when(j == 0)
    def _():
        xn_ref[...] = _rms(x_ref[...], g_ref[...]).astype(BF16)
        o_ref[...] = jnp.zeros_like(o_ref)

    xn = xn_ref[...]
    hg = _dot(xn, wg_ref[...])
    hu = _dot(xn, wu_ref[...])
    a = (hg * jax.nn.sigmoid(hg) * hu).astype(BF16)
    o_ref[...] += _dot(a, wd_ref[...])

    @pl.when(j == pl.num_programs(1) - 1)
    def _():
        y = x_ref[...] + 0.5 * o_ref[...]
        if final_norm:
            y = _rms(y, gf_ref[...])
        o_ref[...] = y

    for src, dst in zip(nxt_in, nxt_out):
        dst[...] = src[...].astype(BF16)


def _ffn(x, g, wg, wu, wd, gf=None, nxt=None):
    m, d = x.shape
    f = wg.shape[-1]
    tm = _tile(m, ROW_TILE_CAP, V7X_LANES)
    tf = _tile(f, FF_TILE_CAP, V7X_LANES)
    n_i = m // tm
    in_specs = [
        pl.BlockSpec((tm, d), lambda i, j: (i, 0)),
        pl.BlockSpec((1, d), lambda i, j: (0, 0)),
        pl.BlockSpec((d, tf), lambda i, j: (0, j)),
        pl.BlockSpec((d, tf), lambda i, j: (0, j)),
        pl.BlockSpec((tf, d), lambda i, j: (j, 0)),
    ]
    out_specs = [pl.BlockSpec((tm, d), lambda i, j: (i, 0))]
    out_shape = [jax.ShapeDtypeStruct((m, d), F32)]
    args = [x, g, wg, wu, wd]
    if gf is not None:
        in_specs.append(pl.BlockSpec((1, d), lambda i, j: (0, 0)))
        args.append(gf)
    if nxt is not None:
        ng, nu, nd, ln, kn = nxt
        rb = next(r for r in range(V7X_LANES, d + 1, V7X_LANES) if d % r == 0 and d // r <= n_i)
        last_r, last_c = d // rb - 1, f // tf - 1

        def blk_r(i, j):
            return jnp.minimum(i, last_r)

        def blk_c(i, j):
            return jnp.where(i > last_r, last_c, j)

        in_specs += [
            pl.BlockSpec((None, None, rb, tf), lambda i, j: (ln, kn, blk_r(i, j), blk_c(i, j))),
            pl.BlockSpec((None, None, rb, tf), lambda i, j: (ln, kn, blk_r(i, j), blk_c(i, j))),
            pl.BlockSpec((None, None, tf, rb), lambda i, j: (ln, kn, blk_c(i, j), blk_r(i, j))),
        ]
        out_specs += [
            pl.BlockSpec((rb, tf), lambda i, j: (blk_r(i, j), blk_c(i, j))),
            pl.BlockSpec((rb, tf), lambda i, j: (blk_r(i, j), blk_c(i, j))),
            pl.BlockSpec((tf, rb), lambda i, j: (blk_c(i, j), blk_r(i, j))),
        ]
        out_shape += [jax.ShapeDtypeStruct((d, f), BF16), jax.ShapeDtypeStruct((d, f), BF16),
                      jax.ShapeDtypeStruct((f, d), BF16)]
        args += [ng, nu, nd]
    outs = pl.pallas_call(
        functools.partial(_ffn_body, final_norm=gf is not None, cast_next=nxt is not None),
        grid=(n_i, f // tf),
        in_specs=in_specs,
        out_specs=out_specs,
        out_shape=out_shape,
        scratch_shapes=[pltpu.VMEM((tm, d), BF16)],
        compiler_params=_params("arbitrary", "arbitrary"),
        name="ffn_final" if gf is not None else "ffn",
    )(*args)
    return outs[0], tuple(outs[1:])


def _proj_body(x_ref, g_ref, w_ref, *rest, gates, n_split):
    if gates:
        wgc_ref, wgr_ref, z_ref, gc_ref, gr_ref = rest
    else:
        (z_ref,) = rest
    xn = _rms(x_ref[...], g_ref[...]).astype(BF16)
    if gates:
        gc_ref[...] = _dot(xn, wgc_ref[...])
        gr_ref[...] = _dot_nt(wgr_ref[...], xn)
    n = z_ref.shape[1]
    for k in range(n_split):
        cols = slice(k * (n // n_split), (k + 1) * (n // n_split))
        z_ref[:, cols] = _dot(xn, w_ref[:, cols])


def _proj(x, g, w, jl, n, wgc=None, wgr=None):
    m, d = x.shape
    tm = _tile(m, PROJ_ROW_TILE, V7X_LANES)
    gates = wgc is not None
    in_specs = [
        pl.BlockSpec((tm, d), lambda i: (i, 0)),
        pl.BlockSpec((1, d), lambda i: (0, 0)),
        pl.BlockSpec((None, d, n), lambda i: (jl, 0, 0), pipeline_mode=pl.Buffered(1)),
    ]
    out_specs = [pl.BlockSpec((tm, n), lambda i: (i, 0))]
    out_shape = [jax.ShapeDtypeStruct((m, n), F32)]
    args = [x, g, w]
    if gates:
        gr = wgr.shape[1]
        in_specs += [
            pl.BlockSpec((None, d, V7X_LANES), lambda i: (jl, 0, 0), pipeline_mode=pl.Buffered(1)),
            pl.BlockSpec((None, gr, d), lambda i: (jl, 0, 0), pipeline_mode=pl.Buffered(1)),
        ]
        out_specs += [
            pl.BlockSpec((tm, V7X_LANES), lambda i: (i, 0)),
            pl.BlockSpec((gr, tm), lambda i: (0, i)),
        ]
        out_shape += [jax.ShapeDtypeStruct((m, V7X_LANES), F32), jax.ShapeDtypeStruct((gr, m), F32)]
        args += [wgc, wgr]
    outs = pl.pallas_call(
        functools.partial(_proj_body, gates=gates, n_split=n // _tile(n, COL_TILE_CAP, V7X_LANES)),
        grid=(m // tm,),
        in_specs=in_specs,
        out_specs=out_specs,
        out_shape=out_shape,
        compiler_params=_params("parallel"),
        name="proj_gates" if gates else "proj",
    )(*args)
    return outs if gates else outs[0]


def _outproj_body(x_ref, y_ref, w_ref, o_ref):
    o_ref[...] = x_ref[...] + _dot(y_ref[...], w_ref[...])


def _outproj(x, y, w, jl):
    m, d = x.shape
    kdim = y.shape[1]
    tm = _tile(m, ROW_TILE_CAP, V7X_LANES)
    return pl.pallas_call(
        _outproj_body,
        grid=(m // tm,),
        in_specs=[
            pl.BlockSpec((tm, d), lambda i: (i, 0)),
            pl.BlockSpec((tm, kdim), lambda i: (i, 0)),
            pl.BlockSpec((None, kdim, d), lambda i: (jl, 0, 0), pipeline_mode=pl.Buffered(1)),
        ],
        out_specs=pl.BlockSpec((tm, d), lambda i: (i, 0)),
        out_shape=jax.ShapeDtypeStruct((m, d), F32),
        compiler_params=_params("parallel"),
        name="outproj",
    )(x, y, w)


def _even_body(z_ref, ha_ref, hb_ref, wa_ref, wb_ref, bias_ref, lng_ref,
               lnb_ref, ybuf_ref, y_ref, na_ref, nb_ref, ua_scr, ub_scr, sh_scr, *, tc, wa_len, wb_len, ha, hb):
    del ybuf_ref
    t = pl.program_id(1)
    c = wa_ref.shape[1]
    xa_ref, gb_ref, gc_ref, gv_ref, gg_ref = (z_ref.at[:, k * c:(k + 1) * c] for k in range(5))

    @pl.when(t == 0)
    def _():
        ua_scr[0:ha, :] = ha_ref[...]
        ub_scr[0:hb, :] = hb_ref[...]

    @pl.when(t > 0)
    def _():
        ua_scr[0:ha, :] = ua_scr[tc:tc + ha, :]
        ub_scr[0:hb, :] = ub_scr[tc:tc + hb, :]

    ua_scr[ha:ha + tc, :] = gc_ref[...] * xa_ref[...]
    ub_scr[hb:hb + tc, :] = gv_ref[...] * jax.nn.sigmoid(gg_ref[...])

    def conv(w_ref, u_scr, first):
        n_taps = w_ref.shape[0]
        acc = jnp.zeros((tc, c), F32)
        for r in range(V7X_SUBLANES):
            offs = [o for o in range(first, first + n_taps) if o % V7X_SUBLANES == r]
            if not offs:
                continue
            span = max(offs) - r
            if r == 0:
                src = u_scr
            else:
                sh_scr[0:span + tc, :] = u_scr[r:r + span + tc, :]
                src = sh_scr
            for o in offs:
                acc = acc + w_ref[o - first:o - first + 1, :] * src[o - r:o - r + tc, :]
        return acc

    ca = conv(wa_ref, ua_scr, ha - (wa_len - 1))
    y_ref[:, 0:c] = (gb_ref[...] * ca).astype(y_ref.dtype)

    cb = conv(wb_ref, ub_scr, hb - (wb_len - 1))
    cb = cb + bias_ref[...]
    mu = jnp.mean(cb, axis=-1, keepdims=True)
    dc = cb - mu
    var = jnp.mean(dc * dc, axis=-1, keepdims=True)
    ln = dc * lax.rsqrt(var + EPS) * lng_ref[...] + lnb_ref[...]
    y_ref[:, c:2 * c] = (ln * jax.nn.sigmoid(ln)).astype(y_ref.dtype)

    @pl.when(t == pl.num_programs(1) - 1)
    def _():
        na_ref[...] = ua_scr[ha + tc - (wa_len - 1): ha + tc, :]
        nb_ref[...] = ub_scr[hb + tc - (wb_len - 1): hb + tc, :]


def _even_mixer(z, row_off, n_seq, t_len, hist_a, hist_b, conv_a, conv_b, bias, ln_g, ln_b, ybuf):
    c = conv_a.shape[-1]
    wa_len, wb_len = conv_a.shape[0], conv_b.shape[0]
    ha, hb = hist_a.shape[1], hist_b.shape[1]
    tc = _tile(t_len, CONV_TILE_CAP, V7X_SUBLANES)
    nt = t_len // tc
    assert row_off % tc == 0 and tc >= hb and tc >= ha
    blk0 = row_off // tc

    def whole(a):
        return pl.BlockSpec(a.shape, lambda s, t: (0,) * a.ndim)

    return pl.pallas_call(
        functools.partial(_even_body, tc=tc, wa_len=wa_len, wb_len=wb_len, ha=ha, hb=hb),
        grid=(n_seq, nt),
        in_specs=[pl.BlockSpec((tc, 5 * c), lambda s, t: (blk0 + s * nt + t, 0)),
                  pl.BlockSpec((None, ha, c), lambda s, t: (s, 0, 0)),
                  pl.BlockSpec((None, hb, c), lambda s, t: (s, 0, 0)),
                  whole(conv_a), whole(conv_b), whole(bias), whole(ln_g), whole(ln_b),
                  pl.BlockSpec(memory_space=pl.ANY)],
        out_specs=[pl.BlockSpec((tc, 2 * c), lambda s, t: (blk0 + s * nt + t, 0)),
                   pl.BlockSpec((None, wa_len - 1, c), lambda s, t: (s, 0, 0)),
                   pl.BlockSpec((None, wb_len - 1, c), lambda s, t: (s, 0, 0))],
        out_shape=[jax.ShapeDtypeStruct(ybuf.shape, ybuf.dtype),
                   jax.ShapeDtypeStruct((n_seq, wa_len - 1, c), F32),
                   jax.ShapeDtypeStruct((n_seq, wb_len - 1, c), F32)],
        scratch_shapes=[pltpu.VMEM((ha + tc, c), F32), pltpu.VMEM((hb + tc, c), F32),
                        pltpu.VMEM((max(ha, hb) + tc, c), F32)],
        compiler_params=_params("arbitrary", "arbitrary"),
        input_output_aliases={8: 0},
        name="even_mixer",
    )(z, hist_a, hist_b, conv_a, conv_b, bias, ln_g, ln_b, ybuf)


def _lb_body(logit_ref, lb_ref):
    n = logit_ref.shape[0]
    rows = [logit_ref[j:j + 1, :] for j in range(n)]
    mx = functools.reduce(jnp.maximum, rows)
    es = [jnp.exp(r - mx) for r in rows]
    tot = functools.reduce(lambda a, b: a + b, es)
    first = es[0] / tot
    run = jnp.zeros_like(first)
    for j in range(n):
        run = run + es[j] / tot
        lb_ref[j:j + 1, :] = run - first


def _lower_bounds(logits):
    return pl.pallas_call(
        _lb_body,
        out_shape=jax.ShapeDtypeStruct(logits.shape, F32),
        name="hgrn_lower_bounds",
    )(logits)


def _odd_body(z_ref, gcol_ref, grow_ref,
              bcol_ref, brow_ref, nc_ref, nd_ref, lb_ref, c0_ref, n0_ref, m0_ref, s0_ref, ybuf_ref,
              y_ref, c1_ref, n1_ref, m1_ref, s1_ref, c_scr, n_scr, m_scr, s_scr, *, heads, blk):
    del ybuf_ref
    L = blk
    hd_all = heads * HEAD_DIM
    step = pl.program_id(1)
    qc_ref, kc_ref, vc_ref, oc_ref, qd_ref, fd_ref, id_ref, gd_ref = (
        z_ref.at[:, k * hd_all:(k + 1) * hd_all] for k in range(8))

    @pl.when(step == 0)
    def _():
        c_scr[...] = c0_ref[...]
        n_scr[...] = n0_ref[...]
        m_scr[...] = m0_ref[...]
        s_scr[...] = s0_ref[...]

    row = lax.broadcasted_iota(jnp.int32, (L, L), 0)
    col = lax.broadcasted_iota(jnp.int32, (L, L), 1)
    causal = row >= col
    tri_l = causal.astype(BF16)
    tri_u = (row <= col).astype(BF16)

    def cumsum_rows(x):
        hi, mid, lo = _split3(x)
        return _dot(tri_l, hi) + _dot(tri_l, mid) + _dot(tri_l, lo)

    def cumsum_lanes(x):
        hi, mid, lo = _split3(x)
        return _dot(hi, tri_u) + _dot(mid, tri_u) + _dot(lo, tri_u)

    gcol = gcol_ref[...] + bcol_ref[...]
    grow = grow_ref[...] + brow_ref[...]
    bt_col = cumsum_rows(_log_sigmoid(gcol))
    bt_row = cumsum_lanes(_log_sigmoid(grow))
    scale = HEAD_DIM ** -0.5
    m_all = m_scr[...]
    n_all = n_scr[...]
    hr = range(heads)
    hsl = [slice(h * HEAD_DIM, (h + 1) * HEAD_DIM) for h in hr]
    b_c = [bt_col[:, heads + h:heads + h + 1] for h in hr]
    i_c = [gcol[:, h:h + 1] for h in hr]
    b_r = [bt_row[heads + h:heads + h + 1, :] for h in hr]
    i_r = [grow[h:h + 1, :] for h in hr]
    m_prev = [m_all[h:h + 1, 0:1] for h in hr]
    dmat = [jnp.where(causal, b_c[h] - b_r[h] + i_r[h], -jnp.inf) for h in hr]
    inter = [b_c[h] + m_prev[h] for h in hr]
    m_row = [jnp.maximum(inter[h], jnp.max(dmat[h], axis=-1, keepdims=True)) for h in hr]
    q = [qc_ref[:, hsl[h]] for h in hr]
    k = [kc_ref[:, hsl[h]] * scale for h in hr]
    qb = [q[h].astype(BF16) for h in hr]
    vb = [vc_ref[:, hsl[h]].astype(BF16) for h in hr]
    s_mat = [_dot_nt(qb[h], k[h].astype(BF16)) * jnp.exp(dmat[h] - m_row[h]) for h in hr]
    w_inter = [jnp.exp(inter[h] - m_row[h]) for h in hr]
    n_old = [n_all[h:h + 1, :] for h in hr]
    den = [jnp.sum(s_mat[h], axis=-1, keepdims=True)
           + w_inter[h] * jnp.sum(q[h] * n_old[h], axis=-1, keepdims=True) for h in hr]
    c_old = [c_scr[h] for h in hr]
    num = [_dot(s_mat[h].astype(BF16), vb[h]) + w_inter[h] * _dot(qb[h], c_old[h].astype(BF16)) for h in hr]
    hh = [num[h] / jnp.maximum(jnp.abs(den[h]), jnp.exp(-m_row[h])) for h in hr]
    ms = [jnp.mean(hh[h] * hh[h], axis=-1, keepdims=True) for h in hr]
    for h in hr:
        hn = hh[h] * lax.rsqrt(ms[h] + EPS) * nc_ref[:, hsl[h]]
        y_ref[:, hsl[h]] = (jax.nn.sigmoid(oc_ref[:, hsl[h]]) * hn).astype(y_ref.dtype)
    b_last = [b_r[h][:, L - 1:L] for h in hr]
    m_new = [jnp.maximum(b_last[h] + m_prev[h], jnp.max(b_last[h] - b_r[h] + i_r[h], axis=-1, keepdims=True))
             for h in hr]
    kw = [k[h] * jnp.exp(b_last[h] - b_c[h] + i_c[h] - m_new[h]) for h in hr]
    decay = [jnp.exp(b_last[h] + m_prev[h] - m_new[h]) for h in hr]
    for h in hr:
        c_scr[h] = decay[h] * c_old[h] + _dot_tn(kw[h].astype(BF16), vb[h])
    n_scr[...] = jnp.concatenate([decay[h] * n_old[h] + jnp.sum(kw[h], axis=0, keepdims=True) for h in hr], axis=0)
    m_scr[...] = jnp.concatenate([jnp.broadcast_to(m_new[h], (1, m_scr.shape[1])) for h in hr], axis=0)

    lb = lb_ref[...]
    f = lb + (1.0 - lb) * jax.nn.sigmoid(fd_ref[...])
    b_all = cumsum_rows(jnp.log(f))
    kk_all = 1.0 - f
    qd = qd_ref[...]
    qq_all = qd * jax.nn.sigmoid(qd)
    b_end = b_all[L - 1:L, :]
    q_in = (qq_all * jnp.exp(b_all)).astype(BF16)
    k_out = (kk_all * jnp.exp(b_end - b_all)).astype(BF16)
    s_decay = jnp.exp(b_end)
    vd = [id_ref[:, hsl[h]].astype(BF16) for h in hr]
    st_old = [s_scr[h] for h in hr]
    o_inter = [_dot_nt(q_in[:, hsl[h]], st_old[h].astype(BF16)) for h in hr]
    for h in hr:
        s_scr[h] = st_old[h] * s_decay[:, hsl[h]] + _dot_tn(vd[h], k_out[:, hsl[h]])

    row_hd = lax.broadcasted_iota(jnp.int32, (L, hd_all), 0)
    n8 = L // V7X_SUBLANES
    sub3 = lax.broadcasted_iota(jnp.int32, (n8, V7X_SUBLANES, hd_all), 1)
    b3 = b_all.reshape(n8, V7X_SUBLANES, hd_all)
    a_acc = [None for _ in hr]
    m = L // 2
    while m >= 1:
        if m >= V7X_SUBLANES:
            nb = L // (2 * m)
            pieces = [jnp.broadcast_to(b_all[i * 2 * m + m - 1:i * 2 * m + m, :], (2 * m, hd_all)) for i in range(nb)]
            anchor = jnp.concatenate(pieces, axis=0) if nb > 1 else pieces[0]
        else:
            anchor = None
            for i in range(V7X_SUBLANES // (2 * m)):
                piece = jnp.broadcast_to(b3[:, i * 2 * m + m - 1:i * 2 * m + m, :], b3.shape)
                anchor = piece if anchor is None else jnp.where(sub3 >= i * 2 * m, piece, anchor)
            anchor = anchor.reshape(L, hd_all)
        upper = (row_hd & m) != 0
        x_m = (jnp.where(upper, qq_all, kk_all) * jnp.exp(-jnp.abs(b_all - anchor))).astype(BF16)
        valid = ((row & m) != 0) & ((col & m) == 0) & ((row // (2 * m)) == (col // (2 * m)))
        for h in hr:
            a_m = jnp.where(valid, _dot_nt(x_m[:, hsl[h]], x_m[:, hsl[h]]), 0.0)
            a_acc[h] = a_m if a_acc[h] is None else a_acc[h] + a_m
        m //= 2
    q_0, k_0 = qq_all.astype(BF16), kk_all.astype(BF16)
    for h in hr:
        a_acc[h] = a_acc[h] + jnp.where(row == col, _dot_nt(q_0[:, hsl[h]], k_0[:, hsl[h]]), 0.0)

    o_all = [_dot(a_acc[h].astype(BF16), vd[h]) + o_inter[h] for h in hr]
    ms_d = [jnp.mean(o_all[h] * o_all[h], axis=-1, keepdims=True) for h in hr]
    for h in hr:
        on = o_all[h] * lax.rsqrt(ms_d[h] + EPS) * nd_ref[:, hsl[h]]
        y_ref[:, hd_all + h * HEAD_DIM: hd_all + (h + 1) * HEAD_DIM] = (
            jax.nn.sigmoid(gd_ref[:, hsl[h]]) * on).astype(y_ref.dtype)

    @pl.when(step == pl.num_programs(1) - 1)
    def _():
        c1_ref[...] = c_scr[...]
        n1_ref[...] = n_scr[...]
        m1_ref[...] = m_scr[...]
        s1_ref[...] = s_scr[...]


def _odd_mixer(z, gcol, grow3, row_off, n_seq, t_len, blk, bcol, brow, norm_c, norm_d, lb, c0, n0, m0, s0t, ybuf):
    heads = c0.shape[1]
    hd_all = heads * HEAD_DIM
    nblk = t_len // blk
    assert row_off % blk == 0 and t_len % blk == 0
    blk0 = row_off // blk
    gr = grow3.shape[1]

    def whole(a):
        return pl.BlockSpec(a.shape, lambda s, c: (0,) * a.ndim)

    def per_seq(a):
        return pl.BlockSpec((None,) + a.shape[1:], lambda s, c: (s,) + (0,) * (a.ndim - 1))

    state_shapes = [jax.ShapeDtypeStruct(a.shape, F32) for a in (c0, n0, m0, s0t)]
    return pl.pallas_call(
        functools.partial(_odd_body, heads=heads, blk=blk),
        grid=(n_seq, nblk),
        in_specs=[
            pl.BlockSpec((blk, 8 * hd_all), lambda s, c: (blk0 + s * nblk + c, 0)),
            pl.BlockSpec((blk, V7X_LANES), lambda s, c: (blk0 + s * nblk + c, 0)),
            pl.BlockSpec((None, gr, blk), lambda s, c: (s * nblk + c, 0, 0)),
            whole(bcol), whole(brow), whole(norm_c), whole(norm_d), whole(lb),
            per_seq(c0), per_seq(n0), per_seq(m0), per_seq(s0t), pl.BlockSpec(memory_space=pl.ANY)],
        out_specs=[pl.BlockSpec((blk, 2 * hd_all), lambda s, c: (blk0 + s * nblk + c, 0)),
                   per_seq(c0), per_seq(n0), per_seq(m0), per_seq(s0t)],
        out_shape=[jax.ShapeDtypeStruct(ybuf.shape, ybuf.dtype)] + state_shapes,
        scratch_shapes=[pltpu.VMEM(c0.shape[1:], F32), pltpu.VMEM(n0.shape[1:], F32),
                        pltpu.VMEM(m0.shape[1:], F32), pltpu.VMEM(s0t.shape[1:], F32)],
        compiler_params=_params("arbitrary", "arbitrary"),
        input_output_aliases={12: 0},
        name="odd_mixer",
    )(z, gcol, grow3, bcol, brow, norm_c, norm_d, lb, c0, n0, m0, s0t, ybuf)


def kernel(x_prompt, x_sample, state_conv_a, state_conv_b, state_mlstm_c, state_mlstm_n, state_mlstm_m,
           state_hgrn_s, norm_g, norm_f, ffn_w_gate, ffn_w_up, ffn_w_down, even_w_in, even_conv_a,
           even_conv_b, even_conv_b_bias, even_ln_g, even_ln_b, even_w_out, odd_w_in, odd_bias_i,
           odd_bias_f, odd_norm_c, odd_lb_logits, odd_norm_d, odd_w_out):
    bp, tp, d = x_prompt.shape
    bs, ts, _ = x_sample.shape
    mp, ms = bp * tp, bs * ts
    depth = norm_g.shape[0]
    c_a, c_b = even_conv_a.shape[-1], even_conv_b.shape[-1]
    wa_len, wb_len = even_conv_a.shape[1], even_conv_b.shape[1]
    heads = state_mlstm_c.shape[2]
    hd_all = heads * HEAD_DIM
    assert c_a == c_b and even_w_in.shape[-1] == 5 * c_a
    assert state_mlstm_c.shape[3:] == (HEAD_DIM, HEAD_DIM) and state_hgrn_s.shape[2:] == (heads, HEAD_DIM, HEAD_DIM)
    assert odd_w_in.shape[-1] == 8 * hd_all + 2 * heads and 2 * heads <= V7X_LANES

    x = jnp.concatenate([x_prompt.reshape(mp, d), x_sample.reshape(ms, d)], axis=0)

    w_ffn = (ffn_w_gate[0, 0].astype(BF16), ffn_w_up[0, 0].astype(BF16), ffn_w_down[0, 0].astype(BF16))
    ffn_stacks = (ffn_w_gate.astype(F32), ffn_w_up.astype(F32), ffn_w_down.astype(F32))
    ew_in, ew_out = even_w_in.astype(BF16), even_w_out.astype(BF16)
    ow_in, ow_out = odd_w_in.astype(BF16), odd_w_out.astype(BF16)
    w_gate_cols = odd_w_in[:, :, 8 * hd_all:]
    gr = -(-2 * heads // V7X_SUBLANES) * V7X_SUBLANES
    wgc = jnp.pad(w_gate_cols, ((0, 0), (0, 0), (0, V7X_LANES - 2 * heads))).astype(BF16)
    wgr = jnp.pad(jnp.swapaxes(w_gate_cols, 1, 2), ((0, 0), (0, gr - 2 * heads), (0, 0))).astype(BF16)
    gate_bias = jnp.concatenate([odd_bias_i, odd_bias_f], axis=-1).astype(F32)
    bcol = jnp.pad(gate_bias, ((0, 0), (0, V7X_LANES - 2 * heads)))[:, None, :]
    brow = jnp.pad(gate_bias, ((0, 0), (0, gr - 2 * heads)))[:, :, None]
    lb_all = _lower_bounds(odd_lb_logits.astype(F32))

    ha = -(-(wa_len - 1) // V7X_SUBLANES) * V7X_SUBLANES
    hb = -(-(wb_len - 1) // V7X_SUBLANES) * V7X_SUBLANES

    def hist(state, n, h_rows, w_len):
        if state is None:
            return jnp.zeros((n, h_rows, c_a), F32)
        return jnp.pad(state.astype(F32), ((0, 0), (h_rows - (w_len - 1), 0), (0, 0)))

    blk_p = _pow2_block(tp, CHUNK_CAP)
    blk_s = _pow2_block(ts, CHUNK_CAP)
    groups = ((0, bp, tp, blk_p), (mp, bs, ts, blk_s))

    assert c_a == hd_all
    y = jnp.zeros((mp + ms, 2 * c_a), BF16)
    new_a, new_b, new_c, new_n, new_m, new_s = ([[], []] for _ in range(6))
    for l in range(depth):
        j = l // 2
        x, w_ffn = _ffn(x, norm_g[l, 0][None, :], *w_ffn, nxt=ffn_stacks + (l, 1))
        if l % 2 == 0:
            z = _proj(x, norm_g[l, 1][None, :], ew_in, j, 5 * c_a)
            for gi, (off, n, t_len, _) in enumerate(groups):
                sa = None if gi == 0 else state_conv_a[j]
                sb = None if gi == 0 else state_conv_b[j]
                y, a1, b1 = _even_mixer(
                    z, off, n, t_len, hist(sa, n, ha, wa_len), hist(sb, n, hb, wb_len),
                    even_conv_a[j].astype(F32), even_conv_b[j].astype(F32),
                    even_conv_b_bias[j][None, :], even_ln_g[j][None, :], even_ln_b[j][None, :], y)
                new_a[gi].append(a1)
                new_b[gi].append(b1)
            x = _outproj(x, y, ew_out, j)
        else:
            z, gcol, grow = _proj(x, norm_g[l, 1][None, :], ow_in, j, 8 * hd_all, wgc, wgr)
            for gi, (off, n, t_len, blk) in enumerate(groups):
                grow3 = grow[:, off:off + n * t_len].reshape(gr, n * t_len // blk, blk).transpose(1, 0, 2)
                if gi == 0:
                    c0 = jnp.zeros((n, heads, HEAD_DIM, HEAD_DIM), F32)
                    n0 = jnp.zeros((n, heads, HEAD_DIM), F32)
                    m0 = jnp.zeros((n, heads, V7X_LANES), F32)
                    s0t = jnp.zeros((n, heads, HEAD_DIM, HEAD_DIM), F32)
                else:
                    c0 = state_mlstm_c[j].astype(F32)
                    n0 = state_mlstm_n[j].astype(F32)
                    m0 = jnp.broadcast_to(state_mlstm_m[j].astype(F32)[:, :, None], (n, heads, V7X_LANES))
                    s0t = jnp.swapaxes(state_hgrn_s[j].astype(F32), -1, -2)
                y, c1, n1, m1, s1t = _odd_mixer(
                    z, gcol, grow3, off, n, t_len, blk, bcol[j], brow[j], odd_norm_c[j][None, :],
                    odd_norm_d[j][None, :], lb_all[j][None, :], c0, n0, m0, s0t, y)
                new_c[gi].append(c1)
                new_n[gi].append(n1)
                new_m[gi].append(m1[:, :, 0])
                new_s[gi].append(jnp.swapaxes(s1t, -1, -2))
            x = _outproj(x, y, ow_out, j)
        if l == depth - 1:
            x, _ = _ffn(x, norm_g[l, 2][None, :], *w_ffn, gf=norm_f[None, :])
        else:
            x, w_ffn = _ffn(x, norm_g[l, 2][None, :], *w_ffn, nxt=ffn_stacks + (l + 1, 0))

    y_prompt = x[:mp].reshape(bp, tp, d)
    y_sample = x[mp:].reshape(bs, ts, d)
    st = lambda lst, gi: jnp.stack(lst[gi])
    return (y_prompt, y_sample,
            st(new_a, 0), st(new_b, 0), st(new_c, 0), st(new_n, 0), st(new_m, 0), st(new_s, 0),
            st(new_a, 1), st(new_b, 1), st(new_c, 1), st(new_n, 1), st(new_m, 1), st(new_s, 1))
```

```python
import functools

import jax
import jax.numpy as jnp
from jax import lax
from jax.experimental import pallas as pl
from jax.experimental.pallas import tpu as pltpu

F32 = jnp.float32
BF16 = jnp.bfloat16
EPS = 1e-6

V7X_LANES = 128
V7X_SUBLANES = 8
V7X_VMEM_LIMIT = 56 * 1024 * 1024
HEAD_DIM = 128
ROW_TILE_CAP = 768
FF_TILE_CAP = 512
PROJ_ROW_TILE = 128
COL_TILE_CAP = 1024
CONV_TILE_CAP = 256
CHUNK_CAP = 128


def _tile(n, cap, align):
    if n <= cap:
        return n
    for d in range(cap - cap % align, 0, -align):
        if n % d == 0:
            return d
    raise ValueError(f"no tile for {n} (cap {cap}, align {align})")


def _pow2_block(n, cap):
    b = V7X_SUBLANES
    assert n % b == 0
    while 2 * b <= cap and n % (2 * b) == 0:
        b *= 2
    return b


def _params(*sem):
    return pltpu.CompilerParams(dimension_semantics=sem, vmem_limit_bytes=V7X_VMEM_LIMIT)


def _rms(x, g):
    return x * lax.rsqrt(jnp.mean(x * x, axis=-1, keepdims=True) + EPS) * g


def _dot(a, b):
    return jnp.dot(a, b, preferred_element_type=F32)


def _dot_nt(a, b):
    return lax.dot_general(a, b, (((1,), (1,)), ((), ())), preferred_element_type=F32)


def _dot_tn(a, b):
    return lax.dot_general(a, b, (((0,), (0,)), ((), ())), preferred_element_type=F32)


def _split3(x):
    hi = x.astype(BF16)
    r1 = x - hi.astype(F32)
    mid = r1.astype(BF16)
    lo = (r1 - mid.astype(F32)).astype(BF16)
    return hi, mid, lo


def _log_sigmoid(x):
    return jnp.minimum(x, 0.0) - jnp.log1p(jnp.exp(-jnp.abs(x)))


def _ffn_body(x_ref, g_ref, wg_ref, wu_ref, wd_ref, *rest, final_norm, cast_next):
    rest = list(rest)
    gf_ref = rest.pop(0) if final_norm else None
    nxt_in = [rest.pop(0) for _ in range(3)] if cast_next else []
    o_ref = rest.pop(0)
    nxt_out = [rest.pop(0) for _ in range(3)] if cast_next else []
    (xn_ref,) = rest
    j = pl.program_id(1)

    @pl.when(j == 0)
    def _():
        xn_ref[...] = _rms(x_ref[...], g_ref[...]).astype(BF16)
        o_ref[...] = jnp.zeros_like(o_ref)

    xn = xn_ref[...]
    hg = _dot(xn, wg_ref[...])
    hu = _dot(xn, wu_ref[...])
    a = (hg * jax.nn.sigmoid(hg) * hu).astype(BF16)
    o_ref[...] += _dot(a, wd_ref[...])

    @pl.when(j == pl.num_programs(1) - 1)
    def _():
        y = x_ref[...] + 0.5 * o_ref[...]
        if final_norm:
            y = _rms(y, gf_ref[...])
        o_ref[...] = y

    for src, dst in zip(nxt_in, nxt_out):
        dst[...] = src[...].astype(BF16)


def _ffn(x, g, wg, wu, wd, gf=None, nxt=None):
    m, d = x.shape
    n_j, _, tf = wg.shape
    f = n_j * tf
    tm = _tile(m, ROW_TILE_CAP, V7X_LANES)
    n_i = m // tm
    in_specs = [
        pl.BlockSpec((tm, d), lambda i, j: (i, 0)),
        pl.BlockSpec((1, d), lambda i, j: (0, 0)),
        pl.BlockSpec((None, d, tf), lambda i, j: (j, 0, 0)),
        pl.BlockSpec((None, d, tf), lambda i, j: (j, 0, 0)),
        pl.BlockSpec((tf, d), lambda i, j: (j, 0)),
    ]
    out_specs = [pl.BlockSpec((tm, d), lambda i, j: (i, 0))]
    out_shape = [jax.ShapeDtypeStruct((m, d), F32)]
    args = [x, g, wg, wu, wd]
    if gf is not None:
        in_specs.append(pl.BlockSpec((1, d), lambda i, j: (0, 0)))
        args.append(gf)
    if nxt is not None:
        ng, nu, nd, ln, kn = nxt
        rb = next(r for r in range(V7X_LANES, d + 1, V7X_LANES) if d % r == 0 and d // r <= n_i)
        last_r, last_c = d // rb - 1, f // tf - 1

        def blk_r(i, j):
            return jnp.minimum(i, last_r)

        def blk_c(i, j):
            return jnp.where(i > last_r, last_c, j)

        def blk_lin(i, j):
            return blk_r(i, j) * n_j + blk_c(i, j)

        rows_d = tf // (last_r + 1)
        assert tf % (last_r + 1) == 0 and rows_d % (2 * V7X_SUBLANES) == 0
        in_specs += [
            pl.BlockSpec((None, None, rb, tf), lambda i, j: (ln, kn, blk_r(i, j), blk_c(i, j))),
            pl.BlockSpec((None, None, rb, tf), lambda i, j: (ln, kn, blk_r(i, j), blk_c(i, j))),
            pl.BlockSpec((None, None, rows_d, d), lambda i, j: (ln, kn, blk_lin(i, j), 0)),
        ]
        out_specs += [
            pl.BlockSpec((None, rb, tf), lambda i, j: (blk_c(i, j), blk_r(i, j), 0)),
            pl.BlockSpec((None, rb, tf), lambda i, j: (blk_c(i, j), blk_r(i, j), 0)),
            pl.BlockSpec((rows_d, d), lambda i, j: (blk_lin(i, j), 0)),
        ]
        out_shape += [jax.ShapeDtypeStruct((n_j, d, tf), BF16), jax.ShapeDtypeStruct((n_j, d, tf), BF16),
                      jax.ShapeDtypeStruct((f, d), BF16)]
        args += [ng, nu, nd]
    outs = pl.pallas_call(
        functools.partial(_ffn_body, final_norm=gf is not None, cast_next=nxt is not None),
        grid=(n_i, n_j),
        in_specs=in_specs,
        out_specs=out_specs,
        out_shape=out_shape,
        scratch_shapes=[pltpu.VMEM((tm, d), BF16)],
        compiler_params=_params("arbitrary", "arbitrary"),
        name="ffn_final" if gf is not None else "ffn",
    )(*args)
    return outs[0], tuple(outs[1:])


def _proj_body(x_ref, g_ref, w_ref, *rest, gates, n_split):
    if gates:
        wgc_ref, wgr_ref, z_ref, gc_ref, gr_ref = rest
    else:
        (z_ref,) = rest
    xn = _rms(x_ref[...], g_ref[...]).astype(BF16)
    if gates:
        gc_ref[...] = _dot(xn, wgc_ref[...])
        gr_ref[...] = _dot_nt(wgr_ref[...], xn)
    n = z_ref.shape[1]
    for k in range(n_split):
        cols = slice(k * (n // n_split), (k + 1) * (n // n_split))
        z_ref[:, cols] = _dot(xn, w_ref[:, cols])


def _proj(x, g, w, jl, n, wgc=None, wgr=None):
    m, d = x.shape
    tm = _tile(m, PROJ_ROW_TILE, V7X_LANES)
    gates = wgc is not None
    in_specs = [
        pl.BlockSpec((tm, d), lambda i: (i, 0)),
        pl.BlockSpec((1, d), lambda i: (0, 0)),
        pl.BlockSpec((None, d, n), lambda i: (jl, 0, 0), pipeline_mode=pl.Buffered(1)),
    ]
    out_specs = [pl.BlockSpec((tm, n), lambda i: (i, 0))]
    out_shape = [jax.ShapeDtypeStruct((m, n), F32)]
    args = [x, g, w]
    if gates:
        gr = wgr.shape[1]
        in_specs += [
            pl.BlockSpec((None, d, V7X_LANES), lambda i: (jl, 0, 0), pipeline_mode=pl.Buffered(1)),
            pl.BlockSpec((None, gr, d), lambda i: (jl, 0, 0), pipeline_mode=pl.Buffered(1)),
        ]
        out_specs += [
            pl.BlockSpec((tm, V7X_LANES), lambda i: (i, 0)),
            pl.BlockSpec((gr, tm), lambda i: (0, i)),
        ]
        out_shape += [jax.ShapeDtypeStruct((m, V7X_LANES), F32), jax.ShapeDtypeStruct((gr, m), F32)]
        args += [wgc, wgr]
    outs = pl.pallas_call(
        functools.partial(_proj_body, gates=gates, n_split=n // _tile(n, COL_TILE_CAP, V7X_LANES)),
        grid=(m // tm,),
        in_specs=in_specs,
        out_specs=out_specs,
        out_shape=out_shape,
        compiler_params=_params("parallel"),
        name="proj_gates" if gates else "proj",
    )(*args)
    return outs if gates else outs[0]


def _outproj_body(x_ref, y_ref, w_ref, o_ref):
    o_ref[...] = x_ref[...] + _dot(y_ref[...], w_ref[...])


def _outproj(x, y, w, jl):
    m, d = x.shape
    kdim = y.shape[1]
    tm = _tile(m, ROW_TILE_CAP, V7X_LANES)
    return pl.pallas_call(
        _outproj_body,
        grid=(m // tm,),
        in_specs=[
            pl.BlockSpec((tm, d), lambda i: (i, 0)),
            pl.BlockSpec((tm, kdim), lambda i: (i, 0)),
            pl.BlockSpec((None, kdim, d), lambda i: (jl, 0, 0), pipeline_mode=pl.Buffered(1)),
        ],
        out_specs=pl.BlockSpec((tm, d), lambda i: (i, 0)),
        out_shape=jax.ShapeDtypeStruct((m, d), F32),
        compiler_params=_params("parallel"),
        name="outproj",
    )(x, y, w)


def _even_body(z_ref, ha_ref, hb_ref, wa_ref, wb_ref, bias_ref, lng_ref,
               lnb_ref, ybuf_ref, y_ref, na_ref, nb_ref, ua_scr, ub_scr, sh_scr, *, tc, wa_len, wb_len, ha, hb):
    del ybuf_ref
    t = pl.program_id(1)
    c = wa_ref.shape[1]
    xa_ref, gb_ref, gc_ref, gv_ref, gg_ref = (z_ref.at[:, k * c:(k + 1) * c] for k in range(5))

    @pl.when(t == 0)
    def _():
        ua_scr[0:ha, :] = ha_ref[...]
        ub_scr[0:hb, :] = hb_ref[...]

    @pl.when(t > 0)
    def _():
        ua_scr[0:ha, :] = ua_scr[tc:tc + ha, :]
        ub_scr[0:hb, :] = ub_scr[tc:tc + hb, :]

    ua_scr[ha:ha + tc, :] = gc_ref[...] * xa_ref[...]
    ub_scr[hb:hb + tc, :] = gv_ref[...] * jax.nn.sigmoid(gg_ref[...])

    def conv(w_ref, u_scr, first):
        n_taps = w_ref.shape[0]
        acc = jnp.zeros((tc, c), F32)
        for r in range(V7X_SUBLANES):
            offs = [o for o in range(first, first + n_taps) if o % V7X_SUBLANES == r]
            if not offs:
                continue
            span = max(offs) - r
            if r == 0:
                src = u_scr
            else:
                sh_scr[0:span + tc, :] = u_scr[r:r + span + tc, :]
                src = sh_scr
            for o in offs:
                acc = acc + w_ref[o - first:o - first + 1, :] * src[o - r:o - r + tc, :]
        return acc

    ca = conv(wa_ref, ua_scr, ha - (wa_len - 1))
    y_ref[:, 0:c] = (gb_ref[...] * ca).astype(y_ref.dtype)

    cb = conv(wb_ref, ub_scr, hb - (wb_len - 1))
    cb = cb + bias_ref[...]
    mu = jnp.mean(cb, axis=-1, keepdims=True)
    dc = cb - mu
    var = jnp.mean(dc * dc, axis=-1, keepdims=True)
    ln = dc * lax.rsqrt(var + EPS) * lng_ref[...] + lnb_ref[...]
    y_ref[:, c:2 * c] = (ln * jax.nn.sigmoid(ln)).astype(y_ref.dtype)

    @pl.when(t == pl.num_programs(1) - 1)
    def _():
        na_ref[...] = ua_scr[ha + tc - (wa_len - 1): ha + tc, :]
        nb_ref[...] = ub_scr[hb + tc - (wb_len - 1): hb + tc, :]


def _even_mixer(z, row_off, n_seq, t_len, hist_a, hist_b, conv_a, conv_b, bias, ln_g, ln_b, ybuf):
    c = conv_a.shape[-1]
    wa_len, wb_len = conv_a.shape[0], conv_b.shape[0]
    ha, hb = hist_a.shape[1], hist_b.shape[1]
    tc = _tile(t_len, CONV_TILE_CAP, V7X_SUBLANES)
    nt = t_len // tc
    assert row_off % tc == 0 and tc >= hb and tc >= ha
    blk0 = row_off // tc

    def whole(a):
        return pl.BlockSpec(a.shape, lambda s, t: (0,) * a.ndim)

    return pl.pallas_call(
        functools.partial(_even_body, tc=tc, wa_len=wa_len, wb_len=wb_len, ha=ha, hb=hb),
        grid=(n_seq, nt),
        in_specs=[pl.BlockSpec((tc, 5 * c), lambda s, t: (blk0 + s * nt + t, 0)),
                  pl.BlockSpec((None, ha, c), lambda s, t: (s, 0, 0)),
                  pl.BlockSpec((None, hb, c), lambda s, t: (s, 0, 0)),
                  whole(conv_a), whole(conv_b), whole(bias), whole(ln_g), whole(ln_b),
                  pl.BlockSpec(memory_space=pl.ANY)],
        out_specs=[pl.BlockSpec((tc, 2 * c), lambda s, t: (blk0 + s * nt + t, 0)),
                   pl.BlockSpec((None, wa_len - 1, c), lambda s, t: (s, 0, 0)),
                   pl.BlockSpec((None, wb_len - 1, c), lambda s, t: (s, 0, 0))],
        out_shape=[jax.ShapeDtypeStruct(ybuf.shape, ybuf.dtype),
                   jax.ShapeDtypeStruct((n_seq, wa_len - 1, c), F32),
                   jax.ShapeDtypeStruct((n_seq, wb_len - 1, c), F32)],
        scratch_shapes=[pltpu.VMEM((ha + tc, c), F32), pltpu.VMEM((hb + tc, c), F32),
                        pltpu.VMEM((max(ha, hb) + tc, c), F32)],
        compiler_params=_params("arbitrary", "arbitrary"),
        input_output_aliases={8: 0},
        name="even_mixer",
    )(z, hist_a, hist_b, conv_a, conv_b, bias, ln_g, ln_b, ybuf)


def _lb_body(logit_ref, lb_ref):
    n = logit_ref.shape[0]
    rows = [logit_ref[j:j + 1, :] for j in range(n)]
    mx = functools.reduce(jnp.maximum, rows)
    es = [jnp.exp(r - mx) for r in rows]
    tot = functools.reduce(lambda a, b: a + b, es)
    first = es[0] / tot
    run = jnp.zeros_like(first)
    for j in range(n):
        run = run + es[j] / tot
        lb_ref[j:j + 1, :] = run - first


def _lower_bounds(logits):
    return pl.pallas_call(
        _lb_body,
        out_shape=jax.ShapeDtypeStruct(logits.shape, F32),
        name="hgrn_lower_bounds",
    )(logits)


def _odd_body(z_ref, gcol_ref, grow_ref,
              bcol_ref, brow_ref, nc_ref, nd_ref, lb_ref, c0_ref, n0_ref, m0_ref, s0_ref, ybuf_ref,
              y_ref, c1_ref, n1_ref, m1_ref, s1_ref, c_scr, n_scr, m_scr, s_scr, *, heads, blk):
    del ybuf_ref
    L = blk
    hd_all = heads * HEAD_DIM
    step = pl.program_id(1)
    qc_ref, kc_ref, vc_ref, oc_ref, qd_ref, fd_ref, id_ref, gd_ref = (
        z_ref.at[:, k * hd_all:(k + 1) * hd_all] for k in range(8))

    @pl.when(step == 0)
    def _():
        c_scr[...] = c0_ref[...]
        n_scr[...] = n0_ref[...]
        m_scr[...] = m0_ref[...]
        s_scr[...] = s0_ref[...]

    row = lax.broadcasted_iota(jnp.int32, (L, L), 0)
    col = lax.broadcasted_iota(jnp.int32, (L, L), 1)
    causal = row >= col
    tri_l = causal.astype(BF16)
    tri_u = (row <= col).astype(BF16)

    def cumsum_rows(x):
        hi, mid, lo = _split3(x)
        return _dot(tri_l, hi) + _dot(tri_l, mid) + _dot(tri_l, lo)

    def cumsum_lanes(x):
        hi, mid, lo = _split3(x)
        return _dot(hi, tri_u) + _dot(mid, tri_u) + _dot(lo, tri_u)

    gcol = gcol_ref[...] + bcol_ref[...]
    grow = grow_ref[...] + brow_ref[...]
    bt_col = cumsum_rows(_log_sigmoid(gcol))
    bt_row = cumsum_lanes(_log_sigmoid(grow))
    scale = HEAD_DIM ** -0.5
    m_all = m_scr[...]
    n_all = n_scr[...]
    hr = range(heads)
    hsl = [slice(h * HEAD_DIM, (h + 1) * HEAD_DIM) for h in hr]
    b_c = [bt_col[:, heads + h:heads + h + 1] for h in hr]
    i_c = [gcol[:, h:h + 1] for h in hr]
    b_r = [bt_row[heads + h:heads + h + 1, :] for h in hr]
    i_r = [grow[h:h + 1, :] for h in hr]
    m_prev = [m_all[h:h + 1, 0:1] for h in hr]
    dmat = [jnp.where(causal, b_c[h] - b_r[h] + i_r[h], -jnp.inf) for h in hr]
    inter = [b_c[h] + m_prev[h] for h in hr]
    m_row = [jnp.maximum(inter[h], jnp.max(dmat[h], axis=-1, keepdims=True)) for h in hr]
    q = [qc_ref[:, hsl[h]] for h in hr]
    k = [kc_ref[:, hsl[h]] * scale for h in hr]
    qb = [q[h].astype(BF16) for h in hr]
    vb = [vc_ref[:, hsl[h]].astype(BF16) for h in hr]
    s_mat = [_dot_nt(qb[h], k[h].astype(BF16)) * jnp.exp(dmat[h] - m_row[h]) for h in hr]
    w_inter = [jnp.exp(inter[h] - m_row[h]) for h in hr]
    n_old = [n_all[h:h + 1, :] for h in hr]
    den = [jnp.sum(s_mat[h], axis=-1, keepdims=True)
           + w_inter[h] * jnp.sum(q[h] * n_old[h], axis=-1, keepdims=True) for h in hr]
    c_old = [c_scr[h] for h in hr]
    num = [_dot(s_mat[h].astype(BF16), vb[h]) + w_inter[h] * _dot(qb[h], c_old[h].astype(BF16)) for h in hr]
    hh = [num[h] / jnp.maximum(jnp.abs(den[h]), jnp.exp(-m_row[h])) for h in hr]
    ms = [jnp.mean(hh[h] * hh[h], axis=-1, keepdims=True) for h in hr]
    for h in hr:
        hn = hh[h] * lax.rsqrt(ms[h] + EPS) * nc_ref[:, hsl[h]]
        y_ref[:, hsl[h]] = (jax.nn.sigmoid(oc_ref[:, hsl[h]]) * hn).astype(y_ref.dtype)
    b_last = [b_r[h][:, L - 1:L] for h in hr]
    m_new = [jnp.maximum(b_last[h] + m_prev[h], jnp.max(b_last[h] - b_r[h] + i_r[h], axis=-1, keepdims=True))
             for h in hr]
    kw = [k[h] * jnp.exp(b_last[h] - b_c[h] + i_c[h] - m_new[h]) for h in hr]
    decay = [jnp.exp(b_last[h] + m_prev[h] - m_new[h]) for h in hr]
    for h in hr:
        c_scr[h] = decay[h] * c_old[h] + _dot_tn(kw[h].astype(BF16), vb[h])
    n_scr[...] = jnp.concatenate([decay[h] * n_old[h] + jnp.sum(kw[h], axis=0, keepdims=True) for h in hr], axis=0)
    m_scr[...] = jnp.concatenate([jnp.broadcast_to(m_new[h], (1, m_scr.shape[1])) for h in hr], axis=0)

    lb = lb_ref[...]
    f = lb + (1.0 - lb) * jax.nn.sigmoid(fd_ref[...])
    b_all = cumsum_rows(jnp.log(f))
    kk_all = 1.0 - f
    qd = qd_ref[...]
    qq_all = qd * jax.nn.sigmoid(qd)
    b_end = b_all[L - 1:L, :]
    q_in = (qq_all * jnp.exp(b_all)).astype(BF16)
    k_out = (kk_all * jnp.exp(b_end - b_all)).astype(BF16)
    s_decay = jnp.exp(b_end)
    vd = [id_ref[:, hsl[h]].astype(BF16) for h in hr]
    st_old = [s_scr[h] for h in hr]
    o_inter = [_dot_nt(q_in[:, hsl[h]], st_old[h].astype(BF16)) for h in hr]
    for h in hr:
        s_scr[h] = st_old[h] * s_decay[:, hsl[h]] + _dot_tn(vd[h], k_out[:, hsl[h]])

    row_hd = lax.broadcasted_iota(jnp.int32, (L, hd_all), 0)
    n8 = L // V7X_SUBLANES
    sub3 = lax.broadcasted_iota(jnp.int32, (n8, V7X_SUBLANES, hd_all), 1)
    b3 = b_all.reshape(n8, V7X_SUBLANES, hd_all)
    a_acc = [None for _ in hr]
    m = L // 2
    while m >= 1:
        if m >= V7X_SUBLANES:
            nb = L // (2 * m)
            pieces = [jnp.broadcast_to(b_all[i * 2 * m + m - 1:i * 2 * m + m, :], (2 * m, hd_all)) for i in range(nb)]
            anchor = jnp.concatenate(pieces, axis=0) if nb > 1 else pieces[0]
        else:
            anchor = None
            for i in range(V7X_SUBLANES // (2 * m)):
                piece = jnp.broadcast_to(b3[:, i * 2 * m + m - 1:i * 2 * m + m, :], b3.shape)
                anchor = piece if anchor is None else jnp.where(sub3 >= i * 2 * m, piece, anchor)
            anchor = anchor.reshape(L, hd_all)
        upper = (row_hd & m) != 0
        x_m = (jnp.where(upper, qq_all, kk_all) * jnp.exp(-jnp.abs(b_all - anchor))).astype(BF16)
        valid = ((row & m) != 0) & ((col & m) == 0) & ((row // (2 * m)) == (col // (2 * m)))
        for h in hr:
            a_m = jnp.where(valid, _dot_nt(x_m[:, hsl[h]], x_m[:, hsl[h]]), 0.0)
            a_acc[h] = a_m if a_acc[h] is None else a_acc[h] + a_m
        m //= 2
    q_0, k_0 = qq_all.astype(BF16), kk_all.astype(BF16)
    for h in hr:
        a_acc[h] = a_acc[h] + jnp.where(row == col, _dot_nt(q_0[:, hsl[h]], k_0[:, hsl[h]]), 0.0)

    o_all = [_dot(a_acc[h].astype(BF16), vd[h]) + o_inter[h] for h in hr]
    ms_d = [jnp.mean(o_all[h] * o_all[h], axis=-1, keepdims=True) for h in hr]
    for h in hr:
        on = o_all[h] * lax.rsqrt(ms_d[h] + EPS) * nd_ref[:, hsl[h]]
        y_ref[:, hd_all + h * HEAD_DIM: hd_all + (h + 1) * HEAD_DIM] = (
            jax.nn.sigmoid(gd_ref[:, hsl[h]]) * on).astype(y_ref.dtype)

    @pl.when(step == pl.num_programs(1) - 1)
    def _():
        c1_ref[...] = c_scr[...]
        n1_ref[...] = n_scr[...]
        m1_ref[...] = m_scr[...]
        s1_ref[...] = s_scr[...]


def _odd_mixer(z, gcol, grow3, row_off, n_seq, t_len, blk, bcol, brow, norm_c, norm_d, lb, c0, n0, m0, s0t, ybuf):
    heads = c0.shape[1]
    hd_all = heads * HEAD_DIM
    nblk = t_len // blk
    assert row_off % blk == 0 and t_len % blk == 0
    blk0 = row_off // blk
    gr = grow3.shape[1]

    def whole(a):
        return pl.BlockSpec(a.shape, lambda s, c: (0,) * a.ndim)

    def per_seq(a):
        return pl.BlockSpec((None,) + a.shape[1:], lambda s, c: (s,) + (0,) * (a.ndim - 1))

    state_shapes = [jax.ShapeDtypeStruct(a.shape, F32) for a in (c0, n0, m0, s0t)]
    return pl.pallas_call(
        functools.partial(_odd_body, heads=heads, blk=blk),
        grid=(n_seq, nblk),
        in_specs=[
            pl.BlockSpec((blk, 8 * hd_all), lambda s, c: (blk0 + s * nblk + c, 0)),
            pl.BlockSpec((blk, V7X_LANES), lambda s, c: (blk0 + s * nblk + c, 0)),
            pl.BlockSpec((None, gr, blk), lambda s, c: (s * nblk + c, 0, 0)),
            whole(bcol), whole(brow), whole(norm_c), whole(norm_d), whole(lb),
            per_seq(c0), per_seq(n0), per_seq(m0), per_seq(s0t), pl.BlockSpec(memory_space=pl.ANY)],
        out_specs=[pl.BlockSpec((blk, 2 * hd_all), lambda s, c: (blk0 + s * nblk + c, 0)),
                   per_seq(c0), per_seq(n0), per_seq(m0), per_seq(s0t)],
        out_shape=[jax.ShapeDtypeStruct(ybuf.shape, ybuf.dtype)] + state_shapes,
        scratch_shapes=[pltpu.VMEM(c0.shape[1:], F32), pltpu.VMEM(n0.shape[1:], F32),
                        pltpu.VMEM(m0.shape[1:], F32), pltpu.VMEM(s0t.shape[1:], F32)],
        compiler_params=_params("arbitrary", "arbitrary"),
        input_output_aliases={12: 0},
        name="odd_mixer",
    )(z, gcol, grow3, bcol, brow, norm_c, norm_d, lb, c0, n0, m0, s0t, ybuf)


def kernel(x_prompt, x_sample, state_conv_a, state_conv_b, state_mlstm_c, state_mlstm_n, state_mlstm_m,
           state_hgrn_s, norm_g, norm_f, ffn_w_gate, ffn_w_up, ffn_w_down, even_w_in, even_conv_a,
           even_conv_b, even_conv_b_bias, even_ln_g, even_ln_b, even_w_out, odd_w_in, odd_bias_i,
           odd_bias_f, odd_norm_c, odd_lb_logits, odd_norm_d, odd_w_out):
    bp, tp, d = x_prompt.shape
    bs, ts, _ = x_sample.shape
    mp, ms = bp * tp, bs * ts
    depth = norm_g.shape[0]
    c_a, c_b = even_conv_a.shape[-1], even_conv_b.shape[-1]
    wa_len, wb_len = even_conv_a.shape[1], even_conv_b.shape[1]
    heads = state_mlstm_c.shape[2]
    hd_all = heads * HEAD_DIM
    assert c_a == c_b and even_w_in.shape[-1] == 5 * c_a
    assert state_mlstm_c.shape[3:] == (HEAD_DIM, HEAD_DIM) and state_hgrn_s.shape[2:] == (heads, HEAD_DIM, HEAD_DIM)
    assert odd_w_in.shape[-1] == 8 * hd_all + 2 * heads and 2 * heads <= V7X_LANES

    x = jnp.concatenate([x_prompt.reshape(mp, d), x_sample.reshape(ms, d)], axis=0)

    f_dim = ffn_w_gate.shape[-1]
    tf = _tile(f_dim, FF_TILE_CAP, V7X_LANES)

    def col_tiles(w):
        return w.astype(BF16).reshape(d, f_dim // tf, tf).transpose(1, 0, 2)

    w_ffn = (col_tiles(ffn_w_gate[0, 0]), col_tiles(ffn_w_up[0, 0]), ffn_w_down[0, 0].astype(BF16))
    ffn_stacks = (ffn_w_gate.astype(F32), ffn_w_up.astype(F32), ffn_w_down.astype(F32))
    ew_in, ew_out = even_w_in.astype(BF16), even_w_out.astype(BF16)
    ow_in, ow_out = odd_w_in.astype(BF16), odd_w_out.astype(BF16)
    w_gate_cols = odd_w_in[:, :, 8 * hd_all:]
    gr = -(-2 * heads // V7X_SUBLANES) * V7X_SUBLANES
    wgc = jnp.pad(w_gate_cols, ((0, 0), (0, 0), (0, V7X_LANES - 2 * heads))).astype(BF16)
    wgr = jnp.pad(jnp.swapaxes(w_gate_cols, 1, 2), ((0, 0), (0, gr - 2 * heads), (0, 0))).astype(BF16)
    gate_bias = jnp.concatenate([odd_bias_i, odd_bias_f], axis=-1).astype(F32)
    bcol = jnp.pad(gate_bias, ((0, 0), (0, V7X_LANES - 2 * heads)))[:, None, :]
    brow = jnp.pad(gate_bias, ((0, 0), (0, gr - 2 * heads)))[:, :, None]
    lb_all = _lower_bounds(odd_lb_logits.astype(F32))

    ha = -(-(wa_len - 1) // V7X_SUBLANES) * V7X_SUBLANES
    hb = -(-(wb_len - 1) // V7X_SUBLANES) * V7X_SUBLANES

    def hist(state, n, h_rows, w_len):
        if state is None:
            return jnp.zeros((n, h_rows, c_a), F32)
        return jnp.pad(state.astype(F32), ((0, 0), (h_rows - (w_len - 1), 0), (0, 0)))

    blk_p = _pow2_block(tp, CHUNK_CAP)
    blk_s = _pow2_block(ts, CHUNK_CAP)
    groups = ((0, bp, tp, blk_p), (mp, bs, ts, blk_s))

    assert c_a == hd_all
    y = jnp.zeros((mp + ms, 2 * c_a), BF16)
    new_a, new_b, new_c, new_n, new_m, new_s = ([[], []] for _ in range(6))
    for l in range(depth):
        j = l // 2
        x, w_ffn = _ffn(x, norm_g[l, 0][None, :], *w_ffn, nxt=ffn_stacks + (l, 1))
        if l % 2 == 0:
            z = _proj(x, norm_g[l, 1][None, :], ew_in, j, 5 * c_a)
            for gi, (off, n, t_len, _) in enumerate(groups):
                sa = None if gi == 0 else state_conv_a[j]
                sb = None if gi == 0 else state_conv_b[j]
                y, a1, b1 = _even_mixer(
                    z, off, n, t_len, hist(sa, n, ha, wa_len), hist(sb, n, hb, wb_len),
                    even_conv_a[j].astype(F32), even_conv_b[j].astype(F32),
                    even_conv_b_bias[j][None, :], even_ln_g[j][None, :], even_ln_b[j][None, :], y)
                new_a[gi].append(a1)
                new_b[gi].append(b1)
            x = _outproj(x, y, ew_out, j)
        else:
            z, gcol, grow = _proj(x, norm_g[l, 1][None, :], ow_in, j, 8 * hd_all, wgc, wgr)
            for gi, (off, n, t_len, blk) in enumerate(groups):
                grow3 = grow[:, off:off + n * t_len].reshape(gr, n * t_len // blk, blk).transpose(1, 0, 2)
                if gi == 0:
                    c0 = jnp.zeros((n, heads, HEAD_DIM, HEAD_DIM), F32)
                    n0 = jnp.zeros((n, heads, HEAD_DIM), F32)
                    m0 = jnp.zeros((n, heads, V7X_LANES), F32)
                    s0t = jnp.zeros((n, heads, HEAD_DIM, HEAD_DIM), F32)
                else:
                    c0 = state_mlstm_c[j].astype(F32)
                    n0 = state_mlstm_n[j].astype(F32)
                    m0 = jnp.broadcast_to(state_mlstm_m[j].astype(F32)[:, :, None], (n, heads, V7X_LANES))
                    s0t = jnp.swapaxes(state_hgrn_s[j].astype(F32), -1, -2)
                y, c1, n1, m1, s1t = _odd_mixer(
                    z, gcol, grow3, off, n, t_len, blk, bcol[j], brow[j], odd_norm_c[j][None, :],
                    odd_norm_d[j][None, :], lb_all[j][None, :], c0, n0, m0, s0t, y)
                new_c[gi].append(c1)
                new_n[gi].append(n1)
                new_m[gi].append(m1[:, :, 0])
                new_s[gi].append(jnp.swapaxes(s1t, -1, -2))
            x = _outproj(x, y, ow_out, j)
        if l == depth - 1:
            x, _ = _ffn(x, norm_g[l, 2][None, :], *w_ffn, gf=norm_f[None, :])
        else:
            x, w_ffn = _ffn(x, norm_g[l, 2][None, :], *w_ffn, nxt=ffn_stacks + (l + 1, 0))

    y_prompt = x[:mp].reshape(bp, tp, d)
    y_sample = x[mp:].reshape(bs, ts, d)
    st = lambda lst, gi: jnp.stack(lst[gi])
    return (y_prompt, y_sample,
            st(new_a, 0), st(new_b, 0), st(new_c, 0), st(new_n, 0), st(new_m, 0), st(new_s, 0),
            st(new_a, 1), st(new_b, 1), st(new_c, 1), st(new_n, 1), st(new_m, 1), st(new_s, 1))
```

```python
import functools

import jax
import jax.numpy as jnp
from jax import lax
from jax.experimental import pallas as pl
from jax.experimental.pallas import tpu as pltpu

F32 = jnp.float32
BF16 = jnp.bfloat16
EPS = 1e-6

V7X_LANES = 128
V7X_SUBLANES = 8
V7X_VMEM_LIMIT = 56 * 1024 * 1024
HEAD_DIM = 128
ROW_TILE_CAP = 768
FF_TILE_CAP = 512
PROJ_VMEM_BUDGET = 44 * 1024 * 1024
COL_TILE_CAP = 1024
CONV_TILE_CAP = 512
CHUNK_CAP = 128


def _tile(n, cap, align):
    if n <= cap:
        return n
    for d in range(cap - cap % align, 0, -align):
        if n % d == 0:
            return d
    raise ValueError(f"no tile for {n} (cap {cap}, align {align})")


def _pow2_block(n, cap):
    b = V7X_SUBLANES
    assert n % b == 0
    while 2 * b <= cap and n % (2 * b) == 0:
        b *= 2
    return b


def _params(*sem):
    return pltpu.CompilerParams(dimension_semantics=sem, vmem_limit_bytes=V7X_VMEM_LIMIT)


def _rms(x, g):
    return x * lax.rsqrt(jnp.mean(x * x, axis=-1, keepdims=True) + EPS) * g


def _dot(a, b):
    return jnp.dot(a, b, preferred_element_type=F32)


def _dot_nt(a, b):
    return lax.dot_general(a, b, (((1,), (1,)), ((), ())), preferred_element_type=F32)


def _dot_tn(a, b):
    return lax.dot_general(a, b, (((0,), (0,)), ((), ())), preferred_element_type=F32)


def _split3(x):
    hi = x.astype(BF16)
    r1 = x - hi.astype(F32)
    mid = r1.astype(BF16)
    lo = (r1 - mid.astype(F32)).astype(BF16)
    return hi, mid, lo


def _log_sigmoid(x):
    return jnp.minimum(x, 0.0) - jnp.log1p(jnp.exp(-jnp.abs(x)))


def _ffn_body(x_ref, g_ref, wg_ref, wu_ref, wd_ref, *rest, final_norm, cast_next):
    rest = list(rest)
    gf_ref = rest.pop(0) if final_norm else None
    nxt_in = [rest.pop(0) for _ in range(3)] if cast_next else []
    o_ref = rest.pop(0)
    nxt_out = [rest.pop(0) for _ in range(3)] if cast_next else []
    (xn_ref,) = rest
    j = pl.program_id(1)

    @pl.when(j == 0)
    def _():
        xn_ref[...] = _rms(x_ref[...], g_ref[...]).astype(BF16)
        o_ref[...] = jnp.zeros_like(o_ref)

    xn = xn_ref[...]
    hg = _dot(xn, wg_ref[...])
    hu = _dot(xn, wu_ref[...])
    a = (hg * jax.nn.sigmoid(hg) * hu).astype(BF16)
    o_ref[...] += _dot(a, wd_ref[...])

    @pl.when(j == pl.num_programs(1) - 1)
    def _():
        y = x_ref[...] + 0.5 * o_ref[...]
        if final_norm:
            y = _rms(y, gf_ref[...])
        o_ref[...] = y

    for src, dst in zip(nxt_in, nxt_out):
        dst[...] = src[...].astype(BF16)


def _ffn(x, g, wg, wu, wd, gf=None, nxt=None):
    m, d = x.shape
    f = wg.shape[-1]
    tm = _tile(m, ROW_TILE_CAP, V7X_LANES)
    tf = _tile(f, FF_TILE_CAP, V7X_LANES)
    n_i = m // tm
    in_specs = [
        pl.BlockSpec((tm, d), lambda i, j: (i, 0)),
        pl.BlockSpec((1, d), lambda i, j: (0, 0)),
        pl.BlockSpec((d, tf), lambda i, j: (0, j)),
        pl.BlockSpec((d, tf), lambda i, j: (0, j)),
        pl.BlockSpec((tf, d), lambda i, j: (j, 0)),
    ]
    out_specs = [pl.BlockSpec((tm, d), lambda i, j: (i, 0))]
    out_shape = [jax.ShapeDtypeStruct((m, d), F32)]
    args = [x, g, wg, wu, wd]
    if gf is not None:
        in_specs.append(pl.BlockSpec((1, d), lambda i, j: (0, 0)))
        args.append(gf)
    if nxt is not None:
        ng, nu, nd, ln, kn = nxt
        rb = next(r for r in range(V7X_LANES, d + 1, V7X_LANES) if d % r == 0 and d // r <= n_i)
        last_r, last_c = d // rb - 1, f // tf - 1

        def blk_r(i, j):
            return jnp.minimum(i, last_r)

        def blk_c(i, j):
            return jnp.where(i > last_r, last_c, j)

        in_specs += [
            pl.BlockSpec((None, None, rb, tf), lambda i, j: (ln, kn, blk_r(i, j), blk_c(i, j))),
            pl.BlockSpec((None, None, rb, tf), lambda i, j: (ln, kn, blk_r(i, j), blk_c(i, j))),
            pl.BlockSpec((None, None, tf, rb), lambda i, j: (ln, kn, blk_c(i, j), blk_r(i, j))),
        ]
        out_specs += [
            pl.BlockSpec((rb, tf), lambda i, j: (blk_r(i, j), blk_c(i, j))),
            pl.BlockSpec((rb, tf), lambda i, j: (blk_r(i, j), blk_c(i, j))),
            pl.BlockSpec((tf, rb), lambda i, j: (blk_c(i, j), blk_r(i, j))),
        ]
        out_shape += [jax.ShapeDtypeStruct((d, f), BF16), jax.ShapeDtypeStruct((d, f), BF16),
                      jax.ShapeDtypeStruct((f, d), BF16)]
        args += [ng, nu, nd]
    outs = pl.pallas_call(
        functools.partial(_ffn_body, final_norm=gf is not None, cast_next=nxt is not None),
        grid=(n_i, f // tf),
        in_specs=in_specs,
        out_specs=out_specs,
        out_shape=out_shape,
        scratch_shapes=[pltpu.VMEM((tm, d), BF16)],
        compiler_params=_params("arbitrary", "arbitrary"),
        name="ffn_final" if gf is not None else "ffn",
    )(*args)
    return outs[0], tuple(outs[1:])


def _proj_body(x_ref, g_ref, w_ref, *rest, gates, n_split):
    if gates:
        wgc_ref, wgr_ref, z_ref, gc_ref, gr_ref = rest
    else:
        (z_ref,) = rest
    xn = _rms(x_ref[...], g_ref[...]).astype(BF16)
    if gates:
        gc_ref[...] = _dot(xn, wgc_ref[...])
        gr_ref[...] = _dot_nt(wgr_ref[...], xn)
    n = z_ref.shape[1]
    for k in range(n_split):
        cols = slice(k * (n // n_split), (k + 1) * (n // n_split))
        z_ref[:, cols] = _dot(xn, w_ref[:, cols])


def _proj(x, g, w, jl, n, wgc=None, wgr=None):
    m, d = x.shape
    row_cap = (PROJ_VMEM_BUDGET - d * n * 2) // (2 * n * 4)
    tm = _tile(m, max(V7X_LANES, row_cap - row_cap % V7X_LANES), V7X_LANES)
    gates = wgc is not None
    in_specs = [
        pl.BlockSpec((tm, d), lambda i: (i, 0)),
        pl.BlockSpec((1, d), lambda i: (0, 0)),
        pl.BlockSpec((None, d, n), lambda i: (jl, 0, 0), pipeline_mode=pl.Buffered(1)),
    ]
    out_specs = [pl.BlockSpec((tm, n), lambda i: (i, 0))]
    out_shape = [jax.ShapeDtypeStruct((m, n), F32)]
    args = [x, g, w]
    if gates:
        gr = wgr.shape[1]
        in_specs += [
            pl.BlockSpec((None, d, V7X_LANES), lambda i: (jl, 0, 0), pipeline_mode=pl.Buffered(1)),
            pl.BlockSpec((None, gr, d), lambda i: (jl, 0, 0), pipeline_mode=pl.Buffered(1)),
        ]
        out_specs += [
            pl.BlockSpec((tm, V7X_LANES), lambda i: (i, 0)),
            pl.BlockSpec((gr, tm), lambda i: (0, i)),
        ]
        out_shape += [jax.ShapeDtypeStruct((m, V7X_LANES), F32), jax.ShapeDtypeStruct((gr, m), F32)]
        args += [wgc, wgr]
    outs = pl.pallas_call(
        functools.partial(_proj_body, gates=gates, n_split=n // _tile(n, COL_TILE_CAP, V7X_LANES)),
        grid=(m // tm,),
        in_specs=in_specs,
        out_specs=out_specs,
        out_shape=out_shape,
        compiler_params=_params("parallel"),
        name="proj_gates" if gates else "proj",
    )(*args)
    return outs if gates else outs[0]


def _outproj_body(x_ref, y_ref, w_ref, o_ref):
    o_ref[...] = x_ref[...] + _dot(y_ref[...], w_ref[...])


def _outproj(x, y, w, jl):
    m, d = x.shape
    kdim = y.shape[1]
    tm = _tile(m, ROW_TILE_CAP, V7X_LANES)
    return pl.pallas_call(
        _outproj_body,
        grid=(m // tm,),
        in_specs=[
            pl.BlockSpec((tm, d), lambda i: (i, 0)),
            pl.BlockSpec((tm, kdim), lambda i: (i, 0)),
            pl.BlockSpec((None, kdim, d), lambda i: (jl, 0, 0), pipeline_mode=pl.Buffered(1)),
        ],
        out_specs=pl.BlockSpec((tm, d), lambda i: (i, 0)),
        out_shape=jax.ShapeDtypeStruct((m, d), F32),
        compiler_params=_params("parallel"),
        name="outproj",
    )(x, y, w)


def _even_body(z_ref, ha_ref, hb_ref, wa_ref, wb_ref, bias_ref, lng_ref,
               lnb_ref, ybuf_ref, y_ref, na_ref, nb_ref, ua_scr, ub_scr, sh_scr, *, tc, wa_len, wb_len, ha, hb):
    del ybuf_ref
    t = pl.program_id(1)
    c = wa_ref.shape[1]
    xa_ref, gb_ref, gc_ref, gv_ref, gg_ref = (z_ref.at[:, k * c:(k + 1) * c] for k in range(5))

    @pl.when(t == 0)
    def _():
        ua_scr[0:ha, :] = ha_ref[...]
        ub_scr[0:hb, :] = hb_ref[...]

    @pl.when(t > 0)
    def _():
        ua_scr[0:ha, :] = ua_scr[tc:tc + ha, :]
        ub_scr[0:hb, :] = ub_scr[tc:tc + hb, :]

    ua_scr[ha:ha + tc, :] = gc_ref[...] * xa_ref[...]
    ub_scr[hb:hb + tc, :] = gv_ref[...] * jax.nn.sigmoid(gg_ref[...])

    def conv(w_ref, u_scr, first):
        n_taps = w_ref.shape[0]
        acc = jnp.zeros((tc, c), F32)
        for r in range(V7X_SUBLANES):
            offs = [o for o in range(first, first + n_taps) if o % V7X_SUBLANES == r]
            if not offs:
                continue
            span = max(offs) - r
            if r == 0:
                src = u_scr
            else:
                sh_scr[0:span + tc, :] = u_scr[r:r + span + tc, :]
                src = sh_scr
            for o in offs:
                acc = acc + w_ref[o - first:o - first + 1, :] * src[o - r:o - r + tc, :]
        return acc

    ca = conv(wa_ref, ua_scr, ha - (wa_len - 1))
    y_ref[:, 0:c] = (gb_ref[...] * ca).astype(y_ref.dtype)

    cb = conv(wb_ref, ub_scr, hb - (wb_len - 1))
    cb = cb + bias_ref[...]
    mu = jnp.mean(cb, axis=-1, keepdims=True)
    dc = cb - mu
    var = jnp.mean(dc * dc, axis=-1, keepdims=True)
    ln = dc * lax.rsqrt(var + EPS) * lng_ref[...] + lnb_ref[...]
    y_ref[:, c:2 * c] = (ln * jax.nn.sigmoid(ln)).astype(y_ref.dtype)

    @pl.when(t == pl.num_programs(1) - 1)
    def _():
        na_ref[...] = ua_scr[ha + tc - (wa_len - 1): ha + tc, :]
        nb_ref[...] = ub_scr[hb + tc - (wb_len - 1): hb + tc, :]


def _even_mixer(z, row_off, n_seq, t_len, hist_a, hist_b, conv_a, conv_b, bias, ln_g, ln_b, ybuf):
    c = conv_a.shape[-1]
    wa_len, wb_len = conv_a.shape[0], conv_b.shape[0]
    ha, hb = hist_a.shape[1], hist_b.shape[1]
    tc = _tile(t_len, CONV_TILE_CAP, V7X_SUBLANES)
    nt = t_len // tc
    assert row_off % tc == 0 and tc >= hb and tc >= ha
    blk0 = row_off // tc

    def whole(a):
        return pl.BlockSpec(a.shape, lambda s, t: (0,) * a.ndim)

    return pl.pallas_call(
        functools.partial(_even_body, tc=tc, wa_len=wa_len, wb_len=wb_len, ha=ha, hb=hb),
        grid=(n_seq, nt),
        in_specs=[pl.BlockSpec((tc, 5 * c), lambda s, t: (blk0 + s * nt + t, 0)),
                  pl.BlockSpec((None, ha, c), lambda s, t: (s, 0, 0)),
                  pl.BlockSpec((None, hb, c), lambda s, t: (s, 0, 0)),
                  whole(conv_a), whole(conv_b), whole(bias), whole(ln_g), whole(ln_b),
                  pl.BlockSpec(memory_space=pl.ANY)],
        out_specs=[pl.BlockSpec((tc, 2 * c), lambda s, t: (blk0 + s * nt + t, 0)),
                   pl.BlockSpec((None, wa_len - 1, c), lambda s, t: (s, 0, 0)),
                   pl.BlockSpec((None, wb_len - 1, c), lambda s, t: (s, 0, 0))],
        out_shape=[jax.ShapeDtypeStruct(ybuf.shape, ybuf.dtype),
                   jax.ShapeDtypeStruct((n_seq, wa_len - 1, c), F32),
                   jax.ShapeDtypeStruct((n_seq, wb_len - 1, c), F32)],
        scratch_shapes=[pltpu.VMEM((ha + tc, c), F32), pltpu.VMEM((hb + tc, c), F32),
                        pltpu.VMEM((max(ha, hb) + tc, c), F32)],
        compiler_params=_params("arbitrary", "arbitrary"),
        input_output_aliases={8: 0},
        name="even_mixer",
    )(z, hist_a, hist_b, conv_a, conv_b, bias, ln_g, ln_b, ybuf)


def _lb_body(logit_ref, lb_ref):
    n = logit_ref.shape[0]
    rows = [logit_ref[j:j + 1, :] for j in range(n)]
    mx = functools.reduce(jnp.maximum, rows)
    es = [jnp.exp(r - mx) for r in rows]
    tot = functools.reduce(lambda a, b: a + b, es)
    first = es[0] / tot
    run = jnp.zeros_like(first)
    for j in range(n):
        run = run + es[j] / tot
        lb_ref[j:j + 1, :] = run - first


def _lower_bounds(logits):
    return pl.pallas_call(
        _lb_body,
        out_shape=jax.ShapeDtypeStruct(logits.shape, F32),
        name="hgrn_lower_bounds",
    )(logits)


def _odd_body(z_ref, gcol_ref, grow_ref,
              bcol_ref, brow_ref, nc_ref, nd_ref, lb_ref, c0_ref, n0_ref, m0_ref, s0_ref, ybuf_ref,
              y_ref, c1_ref, n1_ref, m1_ref, s1_ref, c_scr, n_scr, m_scr, s_scr, *, heads, blk):
    del ybuf_ref
    L = blk
    hd_all = heads * HEAD_DIM
    step = pl.program_id(1)
    qc_ref, kc_ref, vc_ref, oc_ref, qd_ref, fd_ref, id_ref, gd_ref = (
        z_ref.at[:, k * hd_all:(k + 1) * hd_all] for k in range(8))

    @pl.when(step == 0)
    def _():
        c_scr[...] = c0_ref[...]
        n_scr[...] = n0_ref[...]
        m_scr[...] = m0_ref[...]
        s_scr[...] = s0_ref[...]

    row = lax.broadcasted_iota(jnp.int32, (L, L), 0)
    col = lax.broadcasted_iota(jnp.int32, (L, L), 1)
    causal = row >= col
    tri_l = causal.astype(BF16)
    tri_u = (row <= col).astype(BF16)

    def cumsum_rows(x):
        hi, mid, lo = _split3(x)
        return _dot(tri_l, hi) + _dot(tri_l, mid) + _dot(tri_l, lo)

    def cumsum_lanes(x):
        hi, mid, lo = _split3(x)
        return _dot(hi, tri_u) + _dot(mid, tri_u) + _dot(lo, tri_u)

    gcol = gcol_ref[...] + bcol_ref[...]
    grow = grow_ref[...] + brow_ref[...]
    bt_col = cumsum_rows(_log_sigmoid(gcol))
    bt_row = cumsum_lanes(_log_sigmoid(grow))
    scale = HEAD_DIM ** -0.5
    m_all = m_scr[...]
    n_all = n_scr[...]
    hr = range(heads)
    hsl = [slice(h * HEAD_DIM, (h + 1) * HEAD_DIM) for h in hr]
    b_c = [bt_col[:, heads + h:heads + h + 1] for h in hr]
    i_c = [gcol[:, h:h + 1] for h in hr]
    b_r = [bt_row[heads + h:heads + h + 1, :] for h in hr]
    i_r = [grow[h:h + 1, :] for h in hr]
    m_prev = [m_all[h:h + 1, 0:1] for h in hr]
    dmat = [jnp.where(causal, b_c[h] - b_r[h] + i_r[h], -jnp.inf) for h in hr]
    inter = [b_c[h] + m_prev[h] for h in hr]
    m_row = [jnp.maximum(inter[h], jnp.max(dmat[h], axis=-1, keepdims=True)) for h in hr]
    q = [qc_ref[:, hsl[h]] for h in hr]
    k = [kc_ref[:, hsl[h]] * scale for h in hr]
    qb = [q[h].astype(BF16) for h in hr]
    vb = [vc_ref[:, hsl[h]].astype(BF16) for h in hr]
    s_mat = [_dot_nt(qb[h], k[h].astype(BF16)) * jnp.exp(dmat[h] - m_row[h]) for h in hr]
    w_inter = [jnp.exp(inter[h] - m_row[h]) for h in hr]
    n_old = [n_all[h:h + 1, :] for h in hr]
    den = [jnp.sum(s_mat[h], axis=-1, keepdims=True)
           + w_inter[h] * jnp.sum(q[h] * n_old[h], axis=-1, keepdims=True) for h in hr]
    c_old = [c_scr[h] for h in hr]
    num = [_dot(s_mat[h].astype(BF16), vb[h]) + w_inter[h] * _dot(qb[h], c_old[h].astype(BF16)) for h in hr]
    hh = [num[h] / jnp.maximum(jnp.abs(den[h]), jnp.exp(-m_row[h])) for h in hr]
    ms = [jnp.mean(hh[h] * hh[h], axis=-1, keepdims=True) for h in hr]
    for h in hr:
        hn = hh[h] * lax.rsqrt(ms[h] + EPS) * nc_ref[:, hsl[h]]
        y_ref[:, hsl[h]] = (jax.nn.sigmoid(oc_ref[:, hsl[h]]) * hn).astype(y_ref.dtype)
    b_last = [b_r[h][:, L - 1:L] for h in hr]
    m_new = [jnp.maximum(b_last[h] + m_prev[h], jnp.max(b_last[h] - b_r[h] + i_r[h], axis=-1, keepdims=True))
             for h in hr]
    kw = [k[h] * jnp.exp(b_last[h] - b_c[h] + i_c[h] - m_new[h]) for h in hr]
    decay = [jnp.exp(b_last[h] + m_prev[h] - m_new[h]) for h in hr]
    for h in hr:
        c_scr[h] = decay[h] * c_old[h] + _dot_tn(kw[h].astype(BF16), vb[h])
    n_scr[...] = jnp.concatenate([decay[h] * n_old[h] + jnp.sum(kw[h], axis=0, keepdims=True) for h in hr], axis=0)
    m_scr[...] = jnp.concatenate([jnp.broadcast_to(m_new[h], (1, m_scr.shape[1])) for h in hr], axis=0)

    lb = lb_ref[...]
    f = lb + (1.0 - lb) * jax.nn.sigmoid(fd_ref[...])
    b_all = cumsum_rows(jnp.log(f))
    kk_all = 1.0 - f
    qd = qd_ref[...]
    qq_all = qd * jax.nn.sigmoid(qd)
    b_end = b_all[L - 1:L, :]
    q_in = (qq_all * jnp.exp(b_all)).astype(BF16)
    k_out = (kk_all * jnp.exp(b_end - b_all)).astype(BF16)
    s_decay = jnp.exp(b_end)
    vd = [id_ref[:, hsl[h]].astype(BF16) for h in hr]
    st_old = [s_scr[h] for h in hr]
    o_inter = [_dot_nt(q_in[:, hsl[h]], st_old[h].astype(BF16)) for h in hr]
    for h in hr:
        s_scr[h] = st_old[h] * s_decay[:, hsl[h]] + _dot_tn(vd[h], k_out[:, hsl[h]])

    row_hd = lax.broadcasted_iota(jnp.int32, (L, hd_all), 0)
    n8 = L // V7X_SUBLANES
    sub3 = lax.broadcasted_iota(jnp.int32, (n8, V7X_SUBLANES, hd_all), 1)
    b3 = b_all.reshape(n8, V7X_SUBLANES, hd_all)
    a_acc = [None for _ in hr]
    m = L // 2
    while m >= 1:
        if m >= V7X_SUBLANES:
            nb = L // (2 * m)
            pieces = [jnp.broadcast_to(b_all[i * 2 * m + m - 1:i * 2 * m + m, :], (2 * m, hd_all)) for i in range(nb)]
            anchor = jnp.concatenate(pieces, axis=0) if nb > 1 else pieces[0]
        else:
            anchor = None
            for i in range(V7X_SUBLANES // (2 * m)):
                piece = jnp.broadcast_to(b3[:, i * 2 * m + m - 1:i * 2 * m + m, :], b3.shape)
                anchor = piece if anchor is None else jnp.where(sub3 >= i * 2 * m, piece, anchor)
            anchor = anchor.reshape(L, hd_all)
        upper = (row_hd & m) != 0
        x_m = (jnp.where(upper, qq_all, kk_all) * jnp.exp(-jnp.abs(b_all - anchor))).astype(BF16)
        valid = ((row & m) != 0) & ((col & m) == 0) & ((row // (2 * m)) == (col // (2 * m)))
        for h in hr:
            a_m = jnp.where(valid, _dot_nt(x_m[:, hsl[h]], x_m[:, hsl[h]]), 0.0)
            a_acc[h] = a_m if a_acc[h] is None else a_acc[h] + a_m
        m //= 2
    q_0, k_0 = qq_all.astype(BF16), kk_all.astype(BF16)
    for h in hr:
        a_acc[h] = a_acc[h] + jnp.where(row == col, _dot_nt(q_0[:, hsl[h]], k_0[:, hsl[h]]), 0.0)

    o_all = [_dot(a_acc[h].astype(BF16), vd[h]) + o_inter[h] for h in hr]
    ms_d = [jnp.mean(o_all[h] * o_all[h], axis=-1, keepdims=True) for h in hr]
    for h in hr:
        on = o_all[h] * lax.rsqrt(ms_d[h] + EPS) * nd_ref[:, hsl[h]]
        y_ref[:, hd_all + h * HEAD_DIM: hd_all + (h + 1) * HEAD_DIM] = (
            jax.nn.sigmoid(gd_ref[:, hsl[h]]) * on).astype(y_ref.dtype)

    @pl.when(step == pl.num_programs(1) - 1)
    def _():
        c1_ref[...] = c_scr[...]
        n1_ref[...] = n_scr[...]
        m1_ref[...] = m_scr[...]
        s1_ref[...] = s_scr[...]


def _odd_mixer(z, gcol, grow3, row_off, n_seq, t_len, blk, bcol, brow, norm_c, norm_d, lb, c0, n0, m0, s0t, ybuf):
    heads = c0.shape[1]
    hd_all = heads * HEAD_DIM
    nblk = t_len // blk
    assert row_off % blk == 0 and t_len % blk == 0
    blk0 = row_off // blk
    gr = grow3.shape[1]

    def whole(a):
        return pl.BlockSpec(a.shape, lambda s, c: (0,) * a.ndim)

    def per_seq(a):
        return pl.BlockSpec((None,) + a.shape[1:], lambda s, c: (s,) + (0,) * (a.ndim - 1))

    state_shapes = [jax.ShapeDtypeStruct(a.shape, F32) for a in (c0, n0, m0, s0t)]
    return pl.pallas_call(
        functools.partial(_odd_body, heads=heads, blk=blk),
        grid=(n_seq, nblk),
        in_specs=[
            pl.BlockSpec((blk, 8 * hd_all), lambda s, c: (blk0 + s * nblk + c, 0)),
            pl.BlockSpec((blk, V7X_LANES), lambda s, c: (blk0 + s * nblk + c, 0)),
            pl.BlockSpec((None, gr, blk), lambda s, c: (s * nblk + c, 0, 0)),
            whole(bcol), whole(brow), whole(norm_c), whole(norm_d), whole(lb),
            per_seq(c0), per_seq(n0), per_seq(m0), per_seq(s0t), pl.BlockSpec(memory_space=pl.ANY)],
        out_specs=[pl.BlockSpec((blk, 2 * hd_all), lambda s, c: (blk0 + s * nblk + c, 0)),
                   per_seq(c0), per_seq(n0), per_seq(m0), per_seq(s0t)],
        out_shape=[jax.ShapeDtypeStruct(ybuf.shape, ybuf.dtype)] + state_shapes,
        scratch_shapes=[pltpu.VMEM(c0.shape[1:], F32), pltpu.VMEM(n0.shape[1:], F32),
                        pltpu.VMEM(m0.shape[1:], F32), pltpu.VMEM(s0t.shape[1:], F32)],
        compiler_params=_params("arbitrary", "arbitrary"),
        input_output_aliases={12: 0},
        name="odd_mixer",
    )(z, gcol, grow3, bcol, brow, norm_c, norm_d, lb, c0, n0, m0, s0t, ybuf)


def kernel(x_prompt, x_sample, state_conv_a, state_conv_b, state_mlstm_c, state_mlstm_n, state_mlstm_m,
           state_hgrn_s, norm_g, norm_f, ffn_w_gate, ffn_w_up, ffn_w_down, even_w_in, even_conv_a,
           even_conv_b, even_conv_b_bias, even_ln_g, even_ln_b, even_w_out, odd_w_in, odd_bias_i,
           odd_bias_f, odd_norm_c, odd_lb_logits, odd_norm_d, odd_w_out):
    bp, tp, d = x_prompt.shape
    bs, ts, _ = x_sample.shape
    mp, ms = bp * tp, bs * ts
    depth = norm_g.shape[0]
    c_a, c_b = even_conv_a.shape[-1], even_conv_b.shape[-1]
    wa_len, wb_len = even_conv_a.shape[1], even_conv_b.shape[1]
    heads = state_mlstm_c.shape[2]
    hd_all = heads * HEAD_DIM
    assert c_a == c_b and even_w_in.shape[-1] == 5 * c_a
    assert state_mlstm_c.shape[3:] == (HEAD_DIM, HEAD_DIM) and state_hgrn_s.shape[2:] == (heads, HEAD_DIM, HEAD_DIM)
    assert odd_w_in.shape[-1] == 8 * hd_all + 2 * heads and 2 * heads <= V7X_LANES

    x = jnp.concatenate([x_prompt.reshape(mp, d), x_sample.reshape(ms, d)], axis=0)

    w_ffn = (ffn_w_gate[0, 0].astype(BF16), ffn_w_up[0, 0].astype(BF16), ffn_w_down[0, 0].astype(BF16))
    ffn_stacks = (ffn_w_gate.astype(F32), ffn_w_up.astype(F32), ffn_w_down.astype(F32))
    ew_in, ew_out = even_w_in.astype(BF16), even_w_out.astype(BF16)
    ow_in, ow_out = odd_w_in.astype(BF16), odd_w_out.astype(BF16)
    w_gate_cols = odd_w_in[:, :, 8 * hd_all:]
    gr = -(-2 * heads // V7X_SUBLANES) * V7X_SUBLANES
    wgc = jnp.pad(w_gate_cols, ((0, 0), (0, 0), (0, V7X_LANES - 2 * heads))).astype(BF16)
    wgr = jnp.pad(jnp.swapaxes(w_gate_cols, 1, 2), ((0, 0), (0, gr - 2 * heads), (0, 0))).astype(BF16)
    gate_bias = jnp.concatenate([odd_bias_i, odd_bias_f], axis=-1).astype(F32)
    bcol = jnp.pad(gate_bias, ((0, 0), (0, V7X_LANES - 2 * heads)))[:, None, :]
    brow = jnp.pad(gate_bias, ((0, 0), (0, gr - 2 * heads)))[:, :, None]
    lb_all = _lower_bounds(odd_lb_logits.astype(F32))

    ha = -(-(wa_len - 1) // V7X_SUBLANES) * V7X_SUBLANES
    hb = -(-(wb_len - 1) // V7X_SUBLANES) * V7X_SUBLANES

    def hist(state, n, h_rows, w_len):
        if state is None:
            return jnp.zeros((n, h_rows, c_a), F32)
        return jnp.pad(state.astype(F32), ((0, 0), (h_rows - (w_len - 1), 0), (0, 0)))

    blk_p = _pow2_block(tp, CHUNK_CAP)
    blk_s = _pow2_block(ts, CHUNK_CAP)
    groups = ((0, bp, tp, blk_p), (mp, bs, ts, blk_s))

    assert c_a == hd_all
    y = jnp.zeros((mp + ms, 2 * c_a), BF16)
    new_a, new_b, new_c, new_n, new_m, new_s = ([[], []] for _ in range(6))
    for l in range(depth):
        j = l // 2
        x, w_ffn = _ffn(x, norm_g[l, 0][None, :], *w_ffn, nxt=ffn_stacks + (l, 1))
        if l % 2 == 0:
            z = _proj(x, norm_g[l, 1][None, :], ew_in, j, 5 * c_a)
            for gi, (off, n, t_len, _) in enumerate(groups):
                sa = None if gi == 0 else state_conv_a[j]
                sb = None if gi == 0 else state_conv_b[j]
                y, a1, b1 = _even_mixer(
                    z, off, n, t_len, hist(sa, n, ha, wa_len), hist(sb, n, hb, wb_len),
                    even_conv_a[j].astype(F32), even_conv_b[j].astype(F32),
                    even_conv_b_bias[j][None, :], even_ln_g[j][None, :], even_ln_b[j][None, :], y)
                new_a[gi].append(a1)
                new_b[gi].append(b1)
            x = _outproj(x, y, ew_out, j)
        else:
            z, gcol, grow = _proj(x, norm_g[l, 1][None, :], ow_in, j, 8 * hd_all, wgc, wgr)
            for gi, (off, n, t_len, blk) in enumerate(groups):
                grow3 = grow[:, off:off + n * t_len].reshape(gr, n * t_len // blk, blk).transpose(1, 0, 2)
                if gi == 0:
                    c0 = jnp.zeros((n, heads, HEAD_DIM, HEAD_DIM), F32)
                    n0 = jnp.zeros((n, heads, HEAD_DIM), F32)
                    m0 = jnp.zeros((n, heads, V7X_LANES), F32)
                    s0t = jnp.zeros((n, heads, HEAD_DIM, HEAD_DIM), F32)
                else:
                    c0 = state_mlstm_c[j].astype(F32)
                    n0 = state_mlstm_n[j].astype(F32)
                    m0 = jnp.broadcast_to(state_mlstm_m[j].astype(F32)[:, :, None], (n, heads, V7X_LANES))
                    s0t = jnp.swapaxes(state_hgrn_s[j].astype(F32), -1, -2)
                y, c1, n1, m1, s1t = _odd_mixer(
                    z, gcol, grow3, off, n, t_len, blk, bcol[j], brow[j], odd_norm_c[j][None, :],
                    odd_norm_d[j][None, :], lb_all[j][None, :], c0, n0, m0, s0t, y)
                new_c[gi].append(c1)
                new_n[gi].append(n1)
                new_m[gi].append(m1[:, :, 0])
                new_s[gi].append(jnp.swapaxes(s1t, -1, -2))
            x = _outproj(x, y, ow_out, j)
        if l == depth - 1:
            x, _ = _ffn(x, norm_g[l, 2][None, :], *w_ffn, gf=norm_f[None, :])
        else:
            x, w_ffn = _ffn(x, norm_g[l, 2][None, :], *w_ffn, nxt=ffn_stacks + (l + 1, 0))

    y_prompt = x[:mp].reshape(bp, tp, d)
    y_sample = x[mp:].reshape(bs, ts, d)
    st = lambda lst, gi: jnp.stack(lst[gi])
    return (y_prompt, y_sample,
            st(new_a, 0), st(new_b, 0), st(new_c, 0), st(new_n, 0), st(new_m, 0), st(new_s, 0),
            st(new_a, 1), st(new_b, 1), st(new_c, 1), st(new_n, 1), st(new_m, 1), st(new_s, 1))
```

```python
import functools

import jax
import jax.numpy as jnp
from jax import lax
from jax.experimental import pallas as pl
from jax.experimental.pallas import tpu as pltpu

F32 = jnp.float32
BF16 = jnp.bfloat16
EPS = 1e-6

V7X_LANES = 128
V7X_SUBLANES = 8
V7X_VMEM_LIMIT = 56 * 1024 * 1024
HEAD_DIM = 128
ROW_TILE_CAP = 768
FF_TILE_CAP = 512
PROJ_VMEM_BUDGET = 44 * 1024 * 1024
COL_TILE_CAP = 1024
CONV_TILE_CAP = 512
CHUNK_CAP = 128


def _tile(n, cap, align):
    if n <= cap:
        return n
    for d in range(cap - cap % align, 0, -align):
        if n % d == 0:
            return d
    raise ValueError(f"no tile for {n} (cap {cap}, align {align})")


def _pow2_block(n, cap):
    b = V7X_SUBLANES
    assert n % b == 0
    while 2 * b <= cap and n % (2 * b) == 0:
        b *= 2
    return b


def _params(*sem):
    return pltpu.CompilerParams(dimension_semantics=sem, vmem_limit_bytes=V7X_VMEM_LIMIT)


def _rms(x, g):
    return x * lax.rsqrt(jnp.mean(x * x, axis=-1, keepdims=True) + EPS) * g


def _dot(a, b):
    return jnp.dot(a, b, preferred_element_type=F32)


def _dot_nt(a, b):
    return lax.dot_general(a, b, (((1,), (1,)), ((), ())), preferred_element_type=F32)


def _dot_tn(a, b):
    return lax.dot_general(a, b, (((0,), (0,)), ((), ())), preferred_element_type=F32)


def _split3(x):
    hi = x.astype(BF16)
    r1 = x - hi.astype(F32)
    mid = r1.astype(BF16)
    lo = (r1 - mid.astype(F32)).astype(BF16)
    return hi, mid, lo


def _log_sigmoid(x):
    return jnp.minimum(x, 0.0) - jnp.log1p(jnp.exp(-jnp.abs(x)))


def _ffn_body(x_ref, g_ref, wg_ref, wu_ref, wd_ref, *rest, final_norm, cast_next):
    rest = list(rest)
    gf_ref = rest.pop(0) if final_norm else None
    nxt_in = [rest.pop(0) for _ in range(3)] if cast_next else []
    o_ref = rest.pop(0)
    nxt_out = [rest.pop(0) for _ in range(3)] if cast_next else []
    (xn_ref,) = rest
    j = pl.program_id(1)

    def step(first):
        if first:
            xn = _rms(x_ref[...], g_ref[...]).astype(BF16)
            xn_ref[...] = xn
        else:
            xn = xn_ref[...]
        hg = _dot(xn, wg_ref[...])
        hu = _dot(xn, wu_ref[...])
        a = (hg * jax.nn.sigmoid(hg) * hu).astype(BF16)
        if first:
            o_ref[...] = _dot(a, wd_ref[...])
        else:
            o_ref[...] += _dot(a, wd_ref[...])

    pl.when(j == 0)(functools.partial(step, True))
    pl.when(j > 0)(functools.partial(step, False))

    @pl.when(j == pl.num_programs(1) - 1)
    def _():
        y = x_ref[...] + 0.5 * o_ref[...]
        if final_norm:
            y = _rms(y, gf_ref[...])
        o_ref[...] = y

    for src, dst in zip(nxt_in, nxt_out):
        dst[...] = src[...].astype(BF16)


def _ffn(x, g, wg, wu, wd, gf=None, nxt=None):
    m, d = x.shape
    f = wg.shape[-1]
    tm = _tile(m, ROW_TILE_CAP, V7X_LANES)
    tf = _tile(f, FF_TILE_CAP, V7X_LANES)
    n_i = m // tm
    in_specs = [
        pl.BlockSpec((tm, d), lambda i, j: (i, 0)),
        pl.BlockSpec((1, d), lambda i, j: (0, 0)),
        pl.BlockSpec((d, tf), lambda i, j: (0, j)),
        pl.BlockSpec((d, tf), lambda i, j: (0, j)),
        pl.BlockSpec((tf, d), lambda i, j: (j, 0)),
    ]
    out_specs = [pl.BlockSpec((tm, d), lambda i, j: (i, 0))]
    out_shape = [jax.ShapeDtypeStruct((m, d), F32)]
    args = [x, g, wg, wu, wd]
    if gf is not None:
        in_specs.append(pl.BlockSpec((1, d), lambda i, j: (0, 0)))
        args.append(gf)
    if nxt is not None:
        ng, nu, nd, ln, kn = nxt
        rb = next(r for r in range(V7X_LANES, d + 1, V7X_LANES) if d % r == 0 and d // r <= n_i)
        last_r, last_c = d // rb - 1, f // tf - 1

        def blk_r(i, j):
            return jnp.minimum(i, last_r)

        def blk_c(i, j):
            return jnp.where(i > last_r, last_c, j)

        in_specs += [
            pl.BlockSpec((None, None, rb, tf), lambda i, j: (ln, kn, blk_r(i, j), blk_c(i, j))),
            pl.BlockSpec((None, None, rb, tf), lambda i, j: (ln, kn, blk_r(i, j), blk_c(i, j))),
            pl.BlockSpec((None, None, tf, rb), lambda i, j: (ln, kn, blk_c(i, j), blk_r(i, j))),
        ]
        out_specs += [
            pl.BlockSpec((rb, tf), lambda i, j: (blk_r(i, j), blk_c(i, j))),
            pl.BlockSpec((rb, tf), lambda i, j: (blk_r(i, j), blk_c(i, j))),
            pl.BlockSpec((tf, rb), lambda i, j: (blk_c(i, j), blk_r(i, j))),
        ]
        out_shape += [jax.ShapeDtypeStruct((d, f), BF16), jax.ShapeDtypeStruct((d, f), BF16),
                      jax.ShapeDtypeStruct((f, d), BF16)]
        args += [ng, nu, nd]
    outs = pl.pallas_call(
        functools.partial(_ffn_body, final_norm=gf is not None, cast_next=nxt is not None),
        grid=(n_i, f // tf),
        in_specs=in_specs,
        out_specs=out_specs,
        out_shape=out_shape,
        scratch_shapes=[pltpu.VMEM((tm, d), BF16)],
        compiler_params=_params("arbitrary", "arbitrary"),
        name="ffn_final" if gf is not None else "ffn",
    )(*args)
    return outs[0], tuple(outs[1:])


def _proj_body(x_ref, g_ref, w_ref, *rest, gates, n_split):
    if gates:
        wgc_ref, wgr_ref, z_ref, gc_ref, gr_ref = rest
    else:
        (z_ref,) = rest
    xn = _rms(x_ref[...], g_ref[...]).astype(BF16)
    if gates:
        gc_ref[...] = _dot(xn, wgc_ref[...])
        gr_ref[...] = _dot_nt(wgr_ref[...], xn)
    n = z_ref.shape[1]
    for k in range(n_split):
        cols = slice(k * (n // n_split), (k + 1) * (n // n_split))
        z_ref[:, cols] = _dot(xn, w_ref[:, cols])


def _proj(x, g, w, jl, n, wgc=None, wgr=None):
    m, d = x.shape
    row_cap = (PROJ_VMEM_BUDGET - d * n * 2) // (2 * n * 4)
    tm = _tile(m, max(V7X_LANES, row_cap - row_cap % V7X_LANES), V7X_LANES)
    gates = wgc is not None
    in_specs = [
        pl.BlockSpec((tm, d), lambda i: (i, 0)),
        pl.BlockSpec((1, d), lambda i: (0, 0)),
        pl.BlockSpec((None, d, n), lambda i: (jl, 0, 0), pipeline_mode=pl.Buffered(1)),
    ]
    out_specs = [pl.BlockSpec((tm, n), lambda i: (i, 0))]
    out_shape = [jax.ShapeDtypeStruct((m, n), F32)]
    args = [x, g, w]
    if gates:
        gr = wgr.shape[1]
        in_specs += [
            pl.BlockSpec((None, d, V7X_LANES), lambda i: (jl, 0, 0), pipeline_mode=pl.Buffered(1)),
            pl.BlockSpec((None, gr, d), lambda i: (jl, 0, 0), pipeline_mode=pl.Buffered(1)),
        ]
        out_specs += [
            pl.BlockSpec((tm, V7X_LANES), lambda i: (i, 0)),
            pl.BlockSpec((gr, tm), lambda i: (0, i)),
        ]
        out_shape += [jax.ShapeDtypeStruct((m, V7X_LANES), F32), jax.ShapeDtypeStruct((gr, m), F32)]
        args += [wgc, wgr]
    outs = pl.pallas_call(
        functools.partial(_proj_body, gates=gates, n_split=n // _tile(n, COL_TILE_CAP, V7X_LANES)),
        grid=(m // tm,),
        in_specs=in_specs,
        out_specs=out_specs,
        out_shape=out_shape,
        compiler_params=_params("parallel"),
        name="proj_gates" if gates else "proj",
    )(*args)
    return outs if gates else outs[0]


def _outproj_body(x_ref, y_ref, w_ref, o_ref):
    o_ref[...] = x_ref[...] + _dot(y_ref[...], w_ref[...])


def _outproj(x, y, w, jl):
    m, d = x.shape
    kdim = y.shape[1]
    tm = _tile(m, ROW_TILE_CAP, V7X_LANES)
    return pl.pallas_call(
        _outproj_body,
        grid=(m // tm,),
        in_specs=[
            pl.BlockSpec((tm, d), lambda i: (i, 0)),
            pl.BlockSpec((tm, kdim), lambda i: (i, 0)),
            pl.BlockSpec((None, kdim, d), lambda i: (jl, 0, 0), pipeline_mode=pl.Buffered(1)),
        ],
        out_specs=pl.BlockSpec((tm, d), lambda i: (i, 0)),
        out_shape=jax.ShapeDtypeStruct((m, d), F32),
        compiler_params=_params("parallel"),
        name="outproj",
    )(x, y, w)


def _even_body(z_ref, ha_ref, hb_ref, wa_ref, wb_ref, bias_ref, lng_ref,
               lnb_ref, ybuf_ref, y_ref, na_ref, nb_ref, ua_scr, ub_scr, sh_scr, *, tc, wa_len, wb_len, ha, hb):
    del ybuf_ref
    t = pl.program_id(1)
    c = wa_ref.shape[1]
    xa_ref, gb_ref, gc_ref, gv_ref, gg_ref = (z_ref.at[:, k * c:(k + 1) * c] for k in range(5))

    @pl.when(t == 0)
    def _():
        ua_scr[0:ha, :] = ha_ref[...]
        ub_scr[0:hb, :] = hb_ref[...]

    @pl.when(t > 0)
    def _():
        ua_scr[0:ha, :] = ua_scr[tc:tc + ha, :]
        ub_scr[0:hb, :] = ub_scr[tc:tc + hb, :]

    ua_scr[ha:ha + tc, :] = gc_ref[...] * xa_ref[...]
    ub_scr[hb:hb + tc, :] = gv_ref[...] * jax.nn.sigmoid(gg_ref[...])

    def conv(w_ref, u_scr, first):
        n_taps = w_ref.shape[0]
        acc = jnp.zeros((tc, c), F32)
        for r in range(V7X_SUBLANES):
            offs = [o for o in range(first, first + n_taps) if o % V7X_SUBLANES == r]
            if not offs:
                continue
            span = max(offs) - r
            if r == 0:
                src = u_scr
            else:
                sh_scr[0:span + tc, :] = u_scr[r:r + span + tc, :]
                src = sh_scr
            for o in offs:
                acc = acc + w_ref[o - first:o - first + 1, :] * src[o - r:o - r + tc, :]
        return acc

    ca = conv(wa_ref, ua_scr, ha - (wa_len - 1))
    y_ref[:, 0:c] = (gb_ref[...] * ca).astype(y_ref.dtype)

    cb = conv(wb_ref, ub_scr, hb - (wb_len - 1))
    cb = cb + bias_ref[...]
    mu = jnp.mean(cb, axis=-1, keepdims=True)
    dc = cb - mu
    var = jnp.mean(dc * dc, axis=-1, keepdims=True)
    ln = dc * lax.rsqrt(var + EPS) * lng_ref[...] + lnb_ref[...]
    y_ref[:, c:2 * c] = (ln * jax.nn.sigmoid(ln)).astype(y_ref.dtype)

    @pl.when(t == pl.num_programs(1) - 1)
    def _():
        na_ref[...] = ua_scr[ha + tc - (wa_len - 1): ha + tc, :]
        nb_ref[...] = ub_scr[hb + tc - (wb_len - 1): hb + tc, :]


def _even_mixer(z, row_off, n_seq, t_len, hist_a, hist_b, conv_a, conv_b, bias, ln_g, ln_b, ybuf):
    c = conv_a.shape[-1]
    wa_len, wb_len = conv_a.shape[0], conv_b.shape[0]
    ha, hb = hist_a.shape[1], hist_b.shape[1]
    tc = _tile(t_len, CONV_TILE_CAP, V7X_SUBLANES)
    nt = t_len // tc
    assert row_off % tc == 0 and tc >= hb and tc >= ha
    blk0 = row_off // tc

    def whole(a):
        return pl.BlockSpec(a.shape, lambda s, t: (0,) * a.ndim)

    return pl.pallas_call(
        functools.partial(_even_body, tc=tc, wa_len=wa_len, wb_len=wb_len, ha=ha, hb=hb),
        grid=(n_seq, nt),
        in_specs=[pl.BlockSpec((tc, 5 * c), lambda s, t: (blk0 + s * nt + t, 0)),
                  pl.BlockSpec((None, ha, c), lambda s, t: (s, 0, 0)),
                  pl.BlockSpec((None, hb, c), lambda s, t: (s, 0, 0)),
                  whole(conv_a), whole(conv_b), whole(bias), whole(ln_g), whole(ln_b),
                  pl.BlockSpec(memory_space=pl.ANY)],
        out_specs=[pl.BlockSpec((tc, 2 * c), lambda s, t: (blk0 + s * nt + t, 0)),
                   pl.BlockSpec((None, wa_len - 1, c), lambda s, t: (s, 0, 0)),
                   pl.BlockSpec((None, wb_len - 1, c), lambda s, t: (s, 0, 0))],
        out_shape=[jax.ShapeDtypeStruct(ybuf.shape, ybuf.dtype),
                   jax.ShapeDtypeStruct((n_seq, wa_len - 1, c), F32),
                   jax.ShapeDtypeStruct((n_seq, wb_len - 1, c), F32)],
        scratch_shapes=[pltpu.VMEM((ha + tc, c), F32), pltpu.VMEM((hb + tc, c), F32),
                        pltpu.VMEM((max(ha, hb) + tc, c), F32)],
        compiler_params=_params("arbitrary", "arbitrary"),
        input_output_aliases={8: 0},
        name="even_mixer",
    )(z, hist_a, hist_b, conv_a, conv_b, bias, ln_g, ln_b, ybuf)


def _lb_body(logit_ref, lb_ref):
    n = logit_ref.shape[0]
    rows = [logit_ref[j:j + 1, :] for j in range(n)]
    mx = functools.reduce(jnp.maximum, rows)
    es = [jnp.exp(r - mx) for r in rows]
    tot = functools.reduce(lambda a, b: a + b, es)
    first = es[0] / tot
    run = jnp.zeros_like(first)
    for j in range(n):
        run = run + es[j] / tot
        lb_ref[j:j + 1, :] = run - first


def _lower_bounds(logits):
    return pl.pallas_call(
        _lb_body,
        out_shape=jax.ShapeDtypeStruct(logits.shape, F32),
        name="hgrn_lower_bounds",
    )(logits)


def _odd_body(z_ref, gcol_ref, grow_ref,
              bcol_ref, brow_ref, nc_ref, nd_ref, lb_ref, c0_ref, n0_ref, m0_ref, s0_ref, ybuf_ref,
              y_ref, c1_ref, n1_ref, m1_ref, s1_ref, c_scr, n_scr, m_scr, s_scr, *, heads, blk):
    del ybuf_ref
    L = blk
    hd_all = heads * HEAD_DIM
    step = pl.program_id(1)
    qc_ref, kc_ref, vc_ref, oc_ref, qd_ref, fd_ref, id_ref, gd_ref = (
        z_ref.at[:, k * hd_all:(k + 1) * hd_all] for k in range(8))

    @pl.when(step == 0)
    def _():
        c_scr[...] = c0_ref[...]
        n_scr[...] = n0_ref[...]
        m_scr[...] = m0_ref[...]
        s_scr[...] = s0_ref[...]

    row = lax.broadcasted_iota(jnp.int32, (L, L), 0)
    col = lax.broadcasted_iota(jnp.int32, (L, L), 1)
    causal = row >= col
    tri_l = causal.astype(BF16)
    tri_u = (row <= col).astype(BF16)

    def cumsum_rows(x):
        hi, mid, lo = _split3(x)
        return _dot(tri_l, hi) + _dot(tri_l, mid) + _dot(tri_l, lo)

    def cumsum_lanes(x):
        hi, mid, lo = _split3(x)
        return _dot(hi, tri_u) + _dot(mid, tri_u) + _dot(lo, tri_u)

    gcol = gcol_ref[...] + bcol_ref[...]
    grow = grow_ref[...] + brow_ref[...]
    bt_col = cumsum_rows(_log_sigmoid(gcol))
    bt_row = cumsum_lanes(_log_sigmoid(grow))
    scale = HEAD_DIM ** -0.5
    m_all = m_scr[...]
    n_all = n_scr[...]
    hr = range(heads)
    hsl = [slice(h * HEAD_DIM, (h + 1) * HEAD_DIM) for h in hr]
    b_c = [bt_col[:, heads + h:heads + h + 1] for h in hr]
    i_c = [gcol[:, h:h + 1] for h in hr]
    b_r = [bt_row[heads + h:heads + h + 1, :] for h in hr]
    i_r = [grow[h:h + 1, :] for h in hr]
    m_prev = [m_all[h:h + 1, 0:1] for h in hr]
    dmat = [jnp.where(causal, b_c[h] - b_r[h] + i_r[h], -jnp.inf) for h in hr]
    inter = [b_c[h] + m_prev[h] for h in hr]
    m_row = [jnp.maximum(inter[h], jnp.max(dmat[h], axis=-1, keepdims=True)) for h in hr]
    q = [qc_ref[:, hsl[h]] for h in hr]
    k = [kc_ref[:, hsl[h]] * scale for h in hr]
    qb = [q[h].astype(BF16) for h in hr]
    vb = [vc_ref[:, hsl[h]].astype(BF16) for h in hr]
    s_mat = [_dot_nt(qb[h], k[h].astype(BF16)) * jnp.exp(dmat[h] - m_row[h]) for h in hr]
    w_inter = [jnp.exp(inter[h] - m_row[h]) for h in hr]
    n_old = [n_all[h:h + 1, :] for h in hr]
    den = [jnp.sum(s_mat[h], axis=-1, keepdims=True)
           + w_inter[h] * jnp.sum(q[h] * n_old[h], axis=-1, keepdims=True) for h in hr]
    c_old = [c_scr[h] for h in hr]
    num = [_dot(s_mat[h].astype(BF16), vb[h]) + w_inter[h] * _dot(qb[h], c_old[h].astype(BF16)) for h in hr]
    hh = [num[h] / jnp.maximum(jnp.abs(den[h]), jnp.exp(-m_row[h])) for h in hr]
    ms = [jnp.mean(hh[h] * hh[h], axis=-1, keepdims=True) for h in hr]
    for h in hr:
        hn = hh[h] * lax.rsqrt(ms[h] + EPS) * nc_ref[:, hsl[h]]
        y_ref[:, hsl[h]] = (jax.nn.sigmoid(oc_ref[:, hsl[h]]) * hn).astype(y_ref.dtype)
    b_last = [b_r[h][:, L - 1:L] for h in hr]
    m_new = [jnp.maximum(b_last[h] + m_prev[h], jnp.max(b_last[h] - b_r[h] + i_r[h], axis=-1, keepdims=True))
             for h in hr]
    kw = [k[h] * jnp.exp(b_last[h] - b_c[h] + i_c[h] - m_new[h]) for h in hr]
    decay = [jnp.exp(b_last[h] + m_prev[h] - m_new[h]) for h in hr]
    for h in hr:
        c_scr[h] = decay[h] * c_old[h] + _dot_tn(kw[h].astype(BF16), vb[h])
    n_scr[...] = jnp.concatenate([decay[h] * n_old[h] + jnp.sum(kw[h], axis=0, keepdims=True) for h in hr], axis=0)
    m_scr[...] = jnp.concatenate([jnp.broadcast_to(m_new[h], (1, m_scr.shape[1])) for h in hr], axis=0)

    lb = lb_ref[...]
    f = lb + (1.0 - lb) * jax.nn.sigmoid(fd_ref[...])
    b_all = cumsum_rows(jnp.log(f))
    kk_all = 1.0 - f
    qd = qd_ref[...]
    qq_all = qd * jax.nn.sigmoid(qd)
    b_end = b_all[L - 1:L, :]
    q_in = (qq_all * jnp.exp(b_all)).astype(BF16)
    k_out = (kk_all * jnp.exp(b_end - b_all)).astype(BF16)
    s_decay = jnp.exp(b_end)
    vd = [id_ref[:, hsl[h]].astype(BF16) for h in hr]
    st_old = [s_scr[h] for h in hr]
    o_inter = [_dot_nt(q_in[:, hsl[h]], st_old[h].astype(BF16)) for h in hr]
    for h in hr:
        s_scr[h] = st_old[h] * s_decay[:, hsl[h]] + _dot_tn(vd[h], k_out[:, hsl[h]])

    row_hd = lax.broadcasted_iota(jnp.int32, (L, hd_all), 0)
    n8 = L // V7X_SUBLANES
    sub3 = lax.broadcasted_iota(jnp.int32, (n8, V7X_SUBLANES, hd_all), 1)
    b3 = b_all.reshape(n8, V7X_SUBLANES, hd_all)
    a_acc = [None for _ in hr]
    m = L // 2
    while m >= 1:
        if m >= V7X_SUBLANES:
            nb = L // (2 * m)
            pieces = [jnp.broadcast_to(b_all[i * 2 * m + m - 1:i * 2 * m + m, :], (2 * m, hd_all)) for i in range(nb)]
            anchor = jnp.concatenate(pieces, axis=0) if nb > 1 else pieces[0]
        else:
            anchor = None
            for i in range(V7X_SUBLANES // (2 * m)):
                piece = jnp.broadcast_to(b3[:, i * 2 * m + m - 1:i * 2 * m + m, :], b3.shape)
                anchor = piece if anchor is None else jnp.where(sub3 >= i * 2 * m, piece, anchor)
            anchor = anchor.reshape(L, hd_all)
        upper = (row_hd & m) != 0
        x_m = (jnp.where(upper, qq_all, kk_all) * jnp.exp(-jnp.abs(b_all - anchor))).astype(BF16)
        valid = ((row & m) != 0) & ((col & m) == 0) & ((row // (2 * m)) == (col // (2 * m)))
        for h in hr:
            a_m = jnp.where(valid, _dot_nt(x_m[:, hsl[h]], x_m[:, hsl[h]]), 0.0)
            a_acc[h] = a_m if a_acc[h] is None else a_acc[h] + a_m
        m //= 2
    q_0, k_0 = qq_all.astype(BF16), kk_all.astype(BF16)
    for h in hr:
        a_acc[h] = a_acc[h] + jnp.where(row == col, _dot_nt(q_0[:, hsl[h]], k_0[:, hsl[h]]), 0.0)

    o_all = [_dot(a_acc[h].astype(BF16), vd[h]) + o_inter[h] for h in hr]
    ms_d = [jnp.mean(o_all[h] * o_all[h], axis=-1, keepdims=True) for h in hr]
    for h in hr:
        on = o_all[h] * lax.rsqrt(ms_d[h] + EPS) * nd_ref[:, hsl[h]]
        y_ref[:, hd_all + h * HEAD_DIM: hd_all + (h + 1) * HEAD_DIM] = (
            jax.nn.sigmoid(gd_ref[:, hsl[h]]) * on).astype(y_ref.dtype)

    @pl.when(step == pl.num_programs(1) - 1)
    def _():
        c1_ref[...] = c_scr[...]
        n1_ref[...] = n_scr[...]
        m1_ref[...] = m_scr[...]
        s1_ref[...] = s_scr[...]


def _odd_mixer(z, gcol, grow3, row_off, n_seq, t_len, blk, bcol, brow, norm_c, norm_d, lb, c0, n0, m0, s0t, ybuf):
    heads = c0.shape[1]
    hd_all = heads * HEAD_DIM
    nblk = t_len // blk
    assert row_off % blk == 0 and t_len % blk == 0
    blk0 = row_off // blk
    gr = grow3.shape[1]

    def whole(a):
        return pl.BlockSpec(a.shape, lambda s, c: (0,) * a.ndim)

    def per_seq(a):
        return pl.BlockSpec((None,) + a.shape[1:], lambda s, c: (s,) + (0,) * (a.ndim - 1))

    state_shapes = [jax.ShapeDtypeStruct(a.shape, F32) for a in (c0, n0, m0, s0t)]
    return pl.pallas_call(
        functools.partial(_odd_body, heads=heads, blk=blk),
        grid=(n_seq, nblk),
        in_specs=[
            pl.BlockSpec((blk, 8 * hd_all), lambda s, c: (blk0 + s * nblk + c, 0)),
            pl.BlockSpec((blk, V7X_LANES), lambda s, c: (blk0 + s * nblk + c, 0)),
            pl.BlockSpec((None, gr, blk), lambda s, c: (s * nblk + c, 0, 0)),
            whole(bcol), whole(brow), whole(norm_c), whole(norm_d), whole(lb),
            per_seq(c0), per_seq(n0), per_seq(m0), per_seq(s0t), pl.BlockSpec(memory_space=pl.ANY)],
        out_specs=[pl.BlockSpec((blk, 2 * hd_all), lambda s, c: (blk0 + s * nblk + c, 0)),
                   per_seq(c0), per_seq(n0), per_seq(m0), per_seq(s0t)],
        out_shape=[jax.ShapeDtypeStruct(ybuf.shape, ybuf.dtype)] + state_shapes,
        scratch_shapes=[pltpu.VMEM(c0.shape[1:], F32), pltpu.VMEM(n0.shape[1:], F32),
                        pltpu.VMEM(m0.shape[1:], F32), pltpu.VMEM(s0t.shape[1:], F32)],
        compiler_params=_params("arbitrary", "arbitrary"),
        input_output_aliases={12: 0},
        name="odd_mixer",
    )(z, gcol, grow3, bcol, brow, norm_c, norm_d, lb, c0, n0, m0, s0t, ybuf)


def kernel(x_prompt, x_sample, state_conv_a, state_conv_b, state_mlstm_c, state_mlstm_n, state_mlstm_m,
           state_hgrn_s, norm_g, norm_f, ffn_w_gate, ffn_w_up, ffn_w_down, even_w_in, even_conv_a,
           even_conv_b, even_conv_b_bias, even_ln_g, even_ln_b, even_w_out, odd_w_in, odd_bias_i,
           odd_bias_f, odd_norm_c, odd_lb_logits, odd_norm_d, odd_w_out):
    bp, tp, d = x_prompt.shape
    bs, ts, _ = x_sample.shape
    mp, ms = bp * tp, bs * ts
    depth = norm_g.shape[0]
    c_a, c_b = even_conv_a.shape[-1], even_conv_b.shape[-1]
    wa_len, wb_len = even_conv_a.shape[1], even_conv_b.shape[1]
    heads = state_mlstm_c.shape[2]
    hd_all = heads * HEAD_DIM
    assert c_a == c_b and even_w_in.shape[-1] == 5 * c_a
    assert state_mlstm_c.shape[3:] == (HEAD_DIM, HEAD_DIM) and state_hgrn_s.shape[2:] == (heads, HEAD_DIM, HEAD_DIM)
    assert odd_w_in.shape[-1] == 8 * hd_all + 2 * heads and 2 * heads <= V7X_LANES

    x = jnp.concatenate([x_prompt.reshape(mp, d), x_sample.reshape(ms, d)], axis=0)

    w_ffn = (ffn_w_gate[0, 0].astype(BF16), ffn_w_up[0, 0].astype(BF16), ffn_w_down[0, 0].astype(BF16))
    ffn_stacks = (ffn_w_gate.astype(F32), ffn_w_up.astype(F32), ffn_w_down.astype(F32))
    ew_in, ew_out = even_w_in.astype(BF16), even_w_out.astype(BF16)
    ow_in, ow_out = odd_w_in.astype(BF16), odd_w_out.astype(BF16)
    w_gate_cols = odd_w_in[:, :, 8 * hd_all:]
    gr = -(-2 * heads // V7X_SUBLANES) * V7X_SUBLANES
    wgc = jnp.pad(w_gate_cols, ((0, 0), (0, 0), (0, V7X_LANES - 2 * heads))).astype(BF16)
    wgr = jnp.pad(jnp.swapaxes(w_gate_cols, 1, 2), ((0, 0), (0, gr - 2 * heads), (0, 0))).astype(BF16)
    gate_bias = jnp.concatenate([odd_bias_i, odd_bias_f], axis=-1).astype(F32)
    bcol = jnp.pad(gate_bias, ((0, 0), (0, V7X_LANES - 2 * heads)))[:, None, :]
    brow = jnp.pad(gate_bias, ((0, 0), (0, gr - 2 * heads)))[:, :, None]
    lb_all = _lower_bounds(odd_lb_logits.astype(F32))

    ha = -(-(wa_len - 1) // V7X_SUBLANES) * V7X_SUBLANES
    hb = -(-(wb_len - 1) // V7X_SUBLANES) * V7X_SUBLANES

    def hist(state, n, h_rows, w_len):
        if state is None:
            return jnp.zeros((n, h_rows, c_a), F32)
        return jnp.pad(state.astype(F32), ((0, 0), (h_rows - (w_len - 1), 0), (0, 0)))

    blk_p = _pow2_block(tp, CHUNK_CAP)
    blk_s = _pow2_block(ts, CHUNK_CAP)
    groups = ((0, bp, tp, blk_p), (mp, bs, ts, blk_s))

    assert c_a == hd_all
    y = jnp.zeros((mp + ms, 2 * c_a), BF16)
    new_a, new_b, new_c, new_n, new_m, new_s = ([[], []] for _ in range(6))
    for l in range(depth):
        j = l // 2
        x, w_ffn = _ffn(x, norm_g[l, 0][None, :], *w_ffn, nxt=ffn_stacks + (l, 1))
        if l % 2 == 0:
            z = _proj(x, norm_g[l, 1][None, :], ew_in, j, 5 * c_a)
            for gi, (off, n, t_len, _) in enumerate(groups):
                sa = None if gi == 0 else state_conv_a[j]
                sb = None if gi == 0 else state_conv_b[j]
                y, a1, b1 = _even_mixer(
                    z, off, n, t_len, hist(sa, n, ha, wa_len), hist(sb, n, hb, wb_len),
                    even_conv_a[j].astype(F32), even_conv_b[j].astype(F32),
                    even_conv_b_bias[j][None, :], even_ln_g[j][None, :], even_ln_b[j][None, :], y)
                new_a[gi].append(a1)
                new_b[gi].append(b1)
            x = _outproj(x, y, ew_out, j)
        else:
            z, gcol, grow = _proj(x, norm_g[l, 1][None, :], ow_in, j, 8 * hd_all, wgc, wgr)
            for gi, (off, n, t_len, blk) in enumerate(groups):
                grow3 = grow[:, off:off + n * t_len].reshape(gr, n * t_len // blk, blk).transpose(1, 0, 2)
                if gi == 0:
                    c0 = jnp.zeros((n, heads, HEAD_DIM, HEAD_DIM), F32)
                    n0 = jnp.zeros((n, heads, HEAD_DIM), F32)
                    m0 = jnp.zeros((n, heads, V7X_LANES), F32)
                    s0t = jnp.zeros((n, heads, HEAD_DIM, HEAD_DIM), F32)
                else:
                    c0 = state_mlstm_c[j].astype(F32)
                    n0 = state_mlstm_n[j].astype(F32)
                    m0 = jnp.broadcast_to(state_mlstm_m[j].astype(F32)[:, :, None], (n, heads, V7X_LANES))
                    s0t = jnp.swapaxes(state_hgrn_s[j].astype(F32), -1, -2)
                y, c1, n1, m1, s1t = _odd_mixer(
                    z, gcol, grow3, off, n, t_len, blk, bcol[j], brow[j], odd_norm_c[j][None, :],
                    odd_norm_d[j][None, :], lb_all[j][None, :], c0, n0, m0, s0t, y)
                new_c[gi].append(c1)
                new_n[gi].append(n1)
                new_m[gi].append(m1[:, :, 0])
                new_s[gi].append(jnp.swapaxes(s1t, -1, -2))
            x = _outproj(x, y, ow_out, j)
        if l == depth - 1:
            x, _ = _ffn(x, norm_g[l, 2][None, :], *w_ffn, gf=norm_f[None, :])
        else:
            x, w_ffn = _ffn(x, norm_g[l, 2][None, :], *w_ffn, nxt=ffn_stacks + (l + 1, 0))

    y_prompt = x[:mp].reshape(bp, tp, d)
    y_sample = x[mp:].reshape(bs, ts, d)
    st = lambda lst, gi: jnp.stack(lst[gi])
    return (y_prompt, y_sample,
            st(new_a, 0), st(new_b, 0), st(new_c, 0), st(new_n, 0), st(new_m, 0), st(new_s, 0),
            st(new_a, 1), st(new_b, 1), st(new_c, 1), st(new_n, 1), st(new_m, 1), st(new_s, 1))
```
